```python
import jax
import jax.numpy as jnp
from jax import lax
import numpy as np


D_MODEL = 2048
BATCH = 16
SEQ = 2048
DEPTH = 2

EPS = 1e-6
M_HEADS = 4
M_HEAD_DIM = D_MODEL // 16
M_WIDTH = M_HEADS * M_HEAD_DIM
M_CONV = 4
M_CHUNK = 64
M_FGATE_BIAS = 3.0
S_HEADS = 16
S_HEAD_DIM = D_MODEL // 32
S_WIDTH = S_HEADS * S_HEAD_DIM
S_GROUPS = 2
S_STATE = 128
S_CONV = 4
S_CHUNK = 128
S_XBC = S_WIDTH + 2 * S_GROUPS * S_STATE
A_HEADS = 4
A_HEAD_DIM = D_MODEL // 16
A_WIDTH = A_HEADS * A_HEAD_DIM
A_BLOCK = 256
A_TOPK = 3
A_QCHUNK = 32
D_MIX = M_WIDTH + S_WIDTH + A_WIDTH
PROJ_SIZES = (M_WIDTH, M_WIDTH, M_WIDTH, M_WIDTH, M_HEADS, M_HEADS,
              S_WIDTH, S_XBC, S_HEADS,
              A_WIDTH, A_WIDTH, A_WIDTH)
D_PROJ = sum(PROJ_SIZES)
D_FF = D_MODEL * 11 // 4
N_EXPERTS = 8
TOP_K = 2
D_FF_EXPERT = D_MODEL * 7 // 2
N_DENSE = (DEPTH + 1) // 2
N_MOE = DEPTH // 2
PLE_DIM = 256

kernel_name = "hybrid_mlstm_ssd_moba_moe_block"

F32 = jnp.float32


def rmsnorm(x, g):
    xf = x.astype(F32)
    y = xf * lax.rsqrt(jnp.mean(xf * xf, axis=-1, keepdims=True) + EPS)
    return (y * g.astype(F32)).astype(x.dtype)


def causal_dwconv(x, w, b):
    k, c = w.shape
    y = lax.conv_general_dilated(x, w[:, None, :].astype(x.dtype), window_strides=(1,),
                                 padding=((k - 1, 0),), dimension_numbers=('NWC', 'WIO', 'NWC'),
                                 feature_group_count=c)
    return y + b.astype(x.dtype)


def mlstm_mixer(q, k, v, o, i_pre, f_pre, conv_w, conv_b, gate_b, norm_g):
    bsz, seq, _ = q.shape
    H, Dh, L = M_HEADS, M_HEAD_DIM, M_CHUNK
    nc = seq // L
    qk = jax.nn.silu(causal_dwconv(jnp.concatenate([q, k], axis=-1), conv_w, conv_b))
    q, k = jnp.split(qk.astype(F32), 2, axis=-1)
    k = k * (Dh ** -0.5)

    def to_chunks(t):
        return t.reshape(bsz, nc, L, H, Dh).transpose(1, 0, 3, 2, 4)

    def gate_chunks(t):
        return t.reshape(bsz, nc, L, H).transpose(1, 0, 3, 2)

    li = gate_chunks(i_pre.astype(F32) + gate_b[:H].astype(F32))
    lf = gate_chunks(jax.nn.log_sigmoid(f_pre.astype(F32) + gate_b[H:].astype(F32)))
    causal = jnp.tril(jnp.ones((L, L), dtype=bool))

    def step(carry, inp):
        C, n, m = carry
        qc, kc, vc, lic, lfc = inp
        b = jnp.cumsum(lfc, axis=-1)
        logw = jnp.where(causal, b[..., :, None] - b[..., None, :] + lic[..., None, :], -jnp.inf)
        m_inter = b + m[..., None]
        m_j = jnp.maximum(m_inter, logw.max(axis=-1))
        w = jnp.exp(logw - m_j[..., None])
        s_inter = jnp.exp(m_inter - m_j)
        sqk = jnp.einsum('bhjd,bhsd->bhjs', qc, kc) * w
        num = s_inter[..., None] * jnp.einsum('bhvk,bhjk->bhjv', C, qc) + jnp.einsum('bhjs,bhsv->bhjv', sqk, vc)
        den = s_inter * jnp.einsum('bhk,bhjk->bhj', n, qc) + sqk.sum(axis=-1)
        h = num / jnp.maximum(jnp.abs(den), jnp.exp(-m_j))[..., None]
        total = b[..., -1]
        lw_end = total[..., None] - b + lic
        m_new = jnp.maximum(total + m, lw_end.max(axis=-1))
        a = jnp.exp(lw_end - m_new[..., None])
        decay = jnp.exp(total + m - m_new)
        C = decay[..., None, None] * C + jnp.einsum('bhs,bhsv,bhsk->bhvk', a, vc, kc)
        n = decay[..., None] * n + jnp.einsum('bhs,bhsk->bhk', a, kc)
        return (C, n, m_new), h

    init = (jnp.zeros((bsz, H, Dh, Dh), F32), jnp.zeros((bsz, H, Dh), F32), jnp.zeros((bsz, H), F32))
    _, h = lax.scan(step, init, (to_chunks(q), to_chunks(k), to_chunks(v.astype(F32)), li, lf))
    h = h.transpose(1, 0, 3, 2, 4).reshape(bsz, seq, H, Dh)
    h = h * lax.rsqrt(jnp.mean(h * h, axis=-1, keepdims=True) + EPS)
    return h.reshape(bsz, seq, M_WIDTH) * norm_g.astype(F32) * jax.nn.sigmoid(o.astype(F32))


def mamba2_mixer(z, xbc, dt_raw, conv_w, conv_b, dt_bias, a_log, d_skip, norm_g):
    bsz, seq, _ = z.shape
    L, G, E, P, N = S_CHUNK, S_GROUPS, S_HEADS // S_GROUPS, S_HEAD_DIM, S_STATE
    nc = seq // L
    xbc = jax.nn.silu(causal_dwconv(xbc, conv_w, conv_b)).astype(F32)
    xs, bm, cm = jnp.split(xbc, [S_WIDTH, S_WIDTH + G * N], axis=-1)
    X = xs.reshape(bsz, nc, L, G, E, P)
    Bc = bm.reshape(bsz, nc, L, G, N)
    Cc = cm.reshape(bsz, nc, L, G, N)
    dt = jax.nn.softplus(dt_raw.astype(F32) + dt_bias.astype(F32)).reshape(bsz, nc, L, G, E)
    a = dt * (-jnp.exp(a_log.astype(F32))).reshape(G, E)
    acum = jnp.cumsum(a, axis=2)
    Xdt = X * dt[..., None]
    causal = jnp.tril(jnp.ones((L, L), dtype=bool))[:, :, None, None]
    seg = acum[:, :, :, None] - acum[:, :, None, :]
    Lmat = jnp.exp(jnp.where(causal, seg, -jnp.inf))
    CB = jnp.einsum('bclgn,bcsgn->bclsg', Cc, Bc)
    y_diag = jnp.einsum('bclsge,bcsgep->bclgep', CB[..., None] * Lmat, Xdt)
    decay_to_end = jnp.exp(acum[:, :, -1:] - acum)
    states = jnp.einsum('bclgn,bclgep->bcgepn', Bc, Xdt * decay_to_end[..., None])
    chunk_decay = jnp.exp(acum[:, :, -1])

    def step(h, inp):
        st, dec = inp
        return h * dec[..., None, None] + st, h

    _, h_in = lax.scan(step, jnp.zeros((bsz, G, E, P, N), F32),
                       (jnp.moveaxis(states, 1, 0), jnp.moveaxis(chunk_decay, 1, 0)))
    h_in = jnp.moveaxis(h_in, 0, 1)
    y_off = jnp.einsum('bclgn,bcgepn->bclgep', Cc, h_in) * jnp.exp(acum)[..., None]
    y = y_diag + y_off + X * d_skip.astype(F32).reshape(G, E, 1)
    gw = S_WIDTH // G
    y = y.reshape(bsz, seq, G, gw) * jax.nn.silu(z.astype(F32)).reshape(bsz, seq, G, gw)
    y = y * lax.rsqrt(jnp.mean(y * y, axis=-1, keepdims=True) + EPS)
    return y.reshape(bsz, seq, S_WIDTH) * norm_g.astype(F32)


def moba_mixer(q, k, v):
    bsz, seq, _ = q.shape
    H, Dh, BS, QC = A_HEADS, A_HEAD_DIM, A_BLOCK, A_QCHUNK
    nb = -(-seq // BS)
    pad = nb * BS - seq
    nqc = seq // QC
    k_sel = max(1, min(A_TOPK, nb - 1))

    def heads(t):
        return t.astype(F32).reshape(bsz, seq, H, Dh).transpose(0, 2, 1, 3)

    q = heads(q) * (Dh ** -0.5)
    kb = jnp.pad(heads(k), ((0, 0), (0, 0), (0, pad), (0, 0))).reshape(bsz, H, nb, BS, Dh)
    vb = jnp.pad(heads(v), ((0, 0), (0, 0), (0, pad), (0, 0))).reshape(bsz, H, nb, BS, Dh)
    k_mean = kb.mean(axis=3)
    q_blk = jnp.arange(seq) // BS
    past = jnp.arange(nb)[None, :] < q_blk[:, None]
    blk_score = jnp.where(past, jnp.einsum('bhsd,bhnd->bhsn', q, k_mean), -jnp.inf)
    _, sel = lax.top_k(blk_score, k_sel)
    valid = jnp.arange(k_sel)[None, :] < q_blk[:, None]
    gather = jax.vmap(jax.vmap(lambda blocks, idx: blocks[idx]))

    def chunk(args):
        qc, selc, validc, c = args
        t0 = c * QC
        bt = t0 // BS
        k_g = gather(kb, selc)
        v_g = gather(vb, selc)
        s_sel = jnp.einsum('bhqd,bhqrkd->bhqrk', qc, k_g)
        s_sel = jnp.where(validc[None, None, :, :, None], s_sel, -jnp.inf).reshape(bsz, H, QC, k_sel * BS)
        k_own = lax.dynamic_index_in_dim(kb, bt, axis=2, keepdims=False)
        v_own = lax.dynamic_index_in_dim(vb, bt, axis=2, keepdims=False)
        own_mask = (bt * BS + jnp.arange(BS))[None, :] <= (t0 + jnp.arange(QC))[:, None]
        s_own = jnp.where(own_mask, jnp.einsum('bhqd,bhkd->bhqk', qc, k_own), -jnp.inf)
        probs = jax.nn.softmax(jnp.concatenate([s_sel, s_own], axis=-1), axis=-1)
        p_sel = probs[..., :k_sel * BS].reshape(bsz, H, QC, k_sel, BS)
        p_own = probs[..., k_sel * BS:]
        return jnp.einsum('bhqrk,bhqrkd->bhqd', p_sel, v_g) + jnp.einsum('bhqk,bhkd->bhqd', p_own, v_own)

    xs = (jnp.moveaxis(q.reshape(bsz, H, nqc, QC, Dh), 2, 0),
          jnp.moveaxis(sel.reshape(bsz, H, nqc, QC, k_sel), 2, 0),
          valid.reshape(nqc, QC, k_sel),
          jnp.arange(nqc))
    out = lax.map(chunk, xs)
    out = jnp.moveaxis(out, 0, 2).reshape(bsz, H, seq, Dh).transpose(0, 2, 1, 3)
    return out.reshape(bsz, seq, A_WIDTH)


def swiglu(x, wg, wu, wd):
    return (jax.nn.silu(x @ wg) * (x @ wu)) @ wd


def moe_swiglu(x, w_router, w_gate, w_up, w_down):
    bsz, seq, d = x.shape
    xt = x.reshape(bsz * seq, d)
    logits = (xt @ w_router).astype(F32)
    top_logit, top_idx = lax.top_k(logits, TOP_K)
    top_w = jax.nn.softmax(top_logit, axis=-1)
    combine = jnp.einsum('tk,tke->te', top_w, jax.nn.one_hot(top_idx, N_EXPERTS, dtype=F32)).astype(x.dtype)
    out = jnp.zeros_like(xt)
    for e in range(N_EXPERTS):
        out = out + combine[:, e:e + 1] * swiglu(xt, w_gate[e], w_up[e], w_down[e])
    return out.reshape(bsz, seq, d)


def setup_inputs(seed: int = 0) -> dict:
    key = jax.random.key(seed)
    ks = iter(jax.random.split(key, 40))

    def nrm(shape, scale):
        return jax.random.normal(next(ks), shape, F32) * scale

    def gain(shape):
        return 1.0 + 0.02 * jax.random.normal(next(ks), shape, F32)

    dt0 = jnp.exp(jax.random.uniform(next(ks), (DEPTH, S_HEADS), F32) * (np.log(0.1) - np.log(0.001)) + np.log(0.001))
    return {
        'x': nrm((BATCH, SEQ, D_MODEL), 1.0),
        'p': nrm((DEPTH, BATCH, SEQ, PLE_DIM), 1.0),
        'ln_mix': gain((DEPTH, D_MODEL)),
        'w_in': nrm((DEPTH, D_MODEL, D_PROJ), D_MODEL ** -0.5),
        'w_out': nrm((DEPTH, D_MIX, D_MODEL), D_MIX ** -0.5),
        'm_conv_w': nrm((DEPTH, M_CONV, 2 * M_WIDTH), M_CONV ** -0.5),
        'm_conv_b': nrm((DEPTH, 2 * M_WIDTH), 0.02),
        'm_gate_b': jnp.concatenate([nrm((DEPTH, M_HEADS), 0.1),
                                     M_FGATE_BIAS + nrm((DEPTH, M_HEADS), 0.5)], axis=-1),
        'm_norm_g': gain((DEPTH, M_WIDTH)),
        's_conv_w': nrm((DEPTH, S_CONV, S_XBC), S_CONV ** -0.5),
        's_conv_b': nrm((DEPTH, S_XBC), 0.02),
        's_dt_bias': dt0 + jnp.log(-jnp.expm1(-dt0)),
        's_a_log': jnp.log(jax.random.uniform(next(ks), (DEPTH, S_HEADS), F32, 1.0, 16.0)),
        's_d': 1.0 + nrm((DEPTH, S_HEADS), 0.1),
        's_norm_g': gain((DEPTH, S_WIDTH)),
        'ln_ffn': gain((DEPTH, D_MODEL)),
        'ffn_w_gate': nrm((N_DENSE, D_MODEL, D_FF), D_MODEL ** -0.5),
        'ffn_w_up': nrm((N_DENSE, D_MODEL, D_FF), D_MODEL ** -0.5),
        'ffn_w_down': nrm((N_DENSE, D_FF, D_MODEL), D_FF ** -0.5),
        'moe_router': nrm((N_MOE, D_MODEL, N_EXPERTS), D_MODEL ** -0.5),
        'moe_w_gate': nrm((N_MOE, N_EXPERTS, D_MODEL, D_FF_EXPERT), D_MODEL ** -0.5),
        'moe_w_up': nrm((N_MOE, N_EXPERTS, D_MODEL, D_FF_EXPERT), D_MODEL ** -0.5),
        'moe_w_down': nrm((N_MOE, N_EXPERTS, D_FF_EXPERT, D_MODEL), D_FF_EXPERT ** -0.5),
        'ln_ple': gain((DEPTH, D_MODEL)),
        'ple_proj': nrm((DEPTH, PLE_DIM, D_MODEL), PLE_DIM ** -0.5),
        'ple_gate': nrm((DEPTH, D_MODEL, D_MODEL), D_MODEL ** -0.5),
        'ln_final': gain((D_MODEL,)),
    }


def reference(x, p, ln_mix, w_in, w_out, m_conv_w, m_conv_b, m_gate_b, m_norm_g,
              s_conv_w, s_conv_b, s_dt_bias, s_a_log, s_d, s_norm_g,
              ln_ffn, ffn_w_gate, ffn_w_up, ffn_w_down,
              moe_router, moe_w_gate, moe_w_up, moe_w_down,
              ln_ple, ple_proj, ple_gate, ln_final):
    split_idx = [int(s) for s in np.cumsum(PROJ_SIZES)[:-1]]
    for i in range(DEPTH):
        u = rmsnorm(x, ln_mix[i]) @ w_in[i]
        (mq, mk, mv, mo, mi, mf, sz, sxbc, sdt, aq, ak, av) = jnp.split(u, split_idx, axis=-1)
        y_m = mlstm_mixer(mq, mk, mv, mo, mi, mf, m_conv_w[i], m_conv_b[i], m_gate_b[i], m_norm_g[i])
        y_s = mamba2_mixer(sz, sxbc, sdt, s_conv_w[i], s_conv_b[i], s_dt_bias[i], s_a_log[i], s_d[i], s_norm_g[i])
        y_a = moba_mixer(aq, ak, av)
        mixed = jnp.concatenate([y_m.astype(x.dtype), y_s.astype(x.dtype), y_a.astype(x.dtype)], axis=-1)
        x = x + mixed @ w_out[i]
        hn = rmsnorm(x, ln_ffn[i])
        if i % 2 == 0:
            j = i // 2
            x = x + swiglu(hn, ffn_w_gate[j], ffn_w_up[j], ffn_w_down[j])
        else:
            j = i // 2
            x = x + moe_swiglu(hn, moe_router[j], moe_w_gate[j], moe_w_up[j], moe_w_down[j])
        gate = jax.nn.sigmoid(rmsnorm(x, ln_ple[i]) @ ple_gate[i])
        x = x + (p[i] @ ple_proj[i]) * gate
    return rmsnorm(x, ln_final)
```

```python
import functools

import jax
import jax.numpy as jnp
from jax import lax
from jax.experimental import pallas as pl
from jax.experimental.pallas import tpu as pltpu

F32 = jnp.float32
BF16 = jnp.bfloat16
HIGHEST = lax.Precision.HIGHEST
NEG_INF = float("-inf")

EPS = 1e-6
LANES = 128
VMEM_LIMIT = 56 * 1024 * 1024

D_MODEL = 2048
M_HEADS = 4
M_HEAD_DIM = 128
M_WIDTH = M_HEADS * M_HEAD_DIM
S_HEADS = 16
S_HEAD_DIM = 64
S_WIDTH = S_HEADS * S_HEAD_DIM
S_GROUPS = 2
S_STATE = 128
S_XBC = S_WIDTH + 2 * S_GROUPS * S_STATE
S_GROUP_WIDTH = S_WIDTH // S_GROUPS
A_HEADS = 4
A_HEAD_DIM = 128
A_WIDTH = A_HEADS * A_HEAD_DIM
A_BLOCK = 256
A_TOPK = 3
N_EXPERTS = 8
CONV_K = 4

COL_MQ = 0
COL_MK = COL_MQ + M_WIDTH
COL_MV = COL_MK + M_WIDTH
COL_MO = COL_MV + M_WIDTH
COL_SZ = COL_MO + M_WIDTH
COL_SXBC = COL_SZ + S_WIDTH
COL_AQ = COL_SXBC + S_XBC
COL_AK = COL_AQ + A_WIDTH
COL_AV = COL_AK + A_WIDTH
COL_GATES = COL_AV + A_WIDTH
D_PROJ_PACKED = COL_GATES + LANES
GATE_LANE_MI = 0
GATE_LANE_MF = M_HEADS
GATE_LANE_SDT = 2 * M_HEADS

ROW_TILE = 512
PROJ_COL_TILE = 896
FF_TILE = 512
M_CHUNK = 128
S_CHUNK = 128
MOE_ROW_TILE = 512
GATHER_TILE = 256


def _params(*sem):
    return pltpu.CompilerParams(dimension_semantics=sem, vmem_limit_bytes=VMEM_LIMIT)


def _rms(x, g):
    return x * lax.rsqrt(jnp.mean(x * x, axis=-1, keepdims=True) + EPS) * g


def _dot(a, b, precision=None):
    return jnp.dot(a, b, preferred_element_type=F32, precision=precision)


def _dot_nt(a, b, precision=None):
    return lax.dot_general(a, b, (((1,), (1,)), ((), ())), preferred_element_type=F32,
                           precision=precision)


def _softplus(x):
    return jnp.maximum(x, 0.0) + jnp.log1p(jnp.exp(-jnp.abs(x)))


def _silu(x):
    return x * jax.nn.sigmoid(x)


def _rmsnorm_kernel(x_ref, g_ref, o_ref):
    o_ref[...] = _rms(x_ref[...], g_ref[...]).astype(o_ref.dtype)


def rmsnorm_bf16(x, g):
    t, d = x.shape
    return pl.pallas_call(
        _rmsnorm_kernel,
        grid=(t // ROW_TILE,),
        in_specs=[pl.BlockSpec((ROW_TILE, d), lambda i: (i, 0)),
                  pl.BlockSpec((1, d), lambda i: (0, 0))],
        out_specs=pl.BlockSpec((ROW_TILE, d), lambda i: (i, 0)),
        out_shape=jax.ShapeDtypeStruct((t, d), BF16),
        compiler_params=_params("arbitrary"),
        name="rmsnorm",
    )(x, g.reshape(1, d))


def _proj_kernel(x_ref, w_ref, o_ref):
    o_ref[...] = _dot(x_ref[...], w_ref[...])


def in_proj(xn, w):
    t, d = xn.shape
    n = w.shape[1]
    return pl.pallas_call(
        _proj_kernel,
        grid=(n // PROJ_COL_TILE, t // ROW_TILE),
        in_specs=[pl.BlockSpec((ROW_TILE, d), lambda j, i: (i, 0)),
                  pl.BlockSpec((d, PROJ_COL_TILE), lambda j, i: (0, j))],
        out_specs=pl.BlockSpec((ROW_TILE, PROJ_COL_TILE), lambda j, i: (i, j)),
        out_shape=jax.ShapeDtypeStruct((t, n), F32),
        compiler_params=_params("arbitrary", "arbitrary"),
        name="in_proj",
    )(xn, w)


def _conv_silu_kernel(u_ref, w_ref, b_ref, s_ref, o_ref):
    x = u_ref[...]
    row = lax.broadcasted_iota(jnp.int32, x.shape, 0)
    y = x * w_ref[CONV_K - 1:CONV_K, :] + b_ref[...]
    for shift in range(1, CONV_K):
        xs = jnp.where(row >= shift, pltpu.roll(x, shift, axis=0), 0.0)
        y = y + xs * w_ref[CONV_K - 1 - shift:CONV_K - shift, :]
    o_ref[...] = (_silu(y) * s_ref[...]).astype(o_ref.dtype)


def conv_silu(u, col0, width, w, b, post_scale, batch, seq):
    ct = 512
    cb0 = col0 // ct
    return pl.pallas_call(
        _conv_silu_kernel,
        grid=(batch, width // ct),
        in_specs=[pl.BlockSpec((seq, ct), lambda b_, c: (b_, cb0 + c)),
                  pl.BlockSpec((CONV_K, ct), lambda b_, c: (0, c)),
                  pl.BlockSpec((1, ct), lambda b_, c: (0, c)),
                  pl.BlockSpec((1, ct), lambda b_, c: (0, c))],
        out_specs=pl.BlockSpec((seq, ct), lambda b_, c: (b_, c)),
        out_shape=jax.ShapeDtypeStruct((batch * seq, width), BF16),
        compiler_params=_params("arbitrary", "arbitrary"),
        name="conv_silu",
    )(u, w, b.reshape(1, width), post_scale.reshape(1, width))


def _mlstm_kernel(q_ref, k_ref, v_ref, o_ref, g_ref, gb_ref, ng_ref, y_ref, ct_ref, n_ref, m_ref):
    c = pl.program_id(1)
    L = q_ref.shape[0]
    H, Dh = M_HEADS, M_HEAD_DIM

    @pl.when(c == 0)
    def _():
        ct_ref[...] = jnp.zeros_like(ct_ref)
        n_ref[...] = jnp.zeros_like(n_ref)
        m_ref[...] = jnp.zeros_like(m_ref)

    g = g_ref[...] + gb_ref[...]
    lf = jnp.minimum(g, 0.0) - jnp.log1p(jnp.exp(-jnp.abs(g)))
    row = lax.broadcasted_iota(jnp.int32, (L, L), 0)
    col = lax.broadcasted_iota(jnp.int32, (L, L), 1)
    causal = row >= col
    tril = jnp.where(causal, 1.0, 0.0).astype(F32)
    bcum = _dot(tril, lf, HIGHEST)
    g_t = g.T
    bcum_t = bcum.T
    for h in range(H):
        li_col = g[:, GATE_LANE_MI + h:GATE_LANE_MI + h + 1]
        li_row = g_t[GATE_LANE_MI + h:GATE_LANE_MI + h + 1, :]
        b_col = bcum[:, GATE_LANE_MF + h:GATE_LANE_MF + h + 1]
        b_row = bcum_t[GATE_LANE_MF + h:GATE_LANE_MF + h + 1, :]
        m_prev = m_ref[h:h + 1, 0:1]
        sl = slice(h * Dh, (h + 1) * Dh)
        qh = q_ref[:, sl]
        kh = k_ref[:, sl]
        vh = v_ref[:, sl].astype(BF16)

        logw = jnp.where(causal, b_col - b_row + li_row, NEG_INF)
        m_inter = b_col + m_prev
        m_j = jnp.maximum(m_inter, jnp.max(logw, axis=1, keepdims=True))
        w = jnp.exp(logw - m_j)
        s_inter = jnp.exp(m_inter - m_j)
        sqk = _dot_nt(qh, kh) * w
        num = s_inter * _dot(qh, ct_ref[h].astype(BF16)) + _dot(sqk.astype(BF16), vh)
        n_row = n_ref[h][0:1, :]
        den = (s_inter * jnp.sum(qh.astype(F32) * n_row, axis=1, keepdims=True)
               + jnp.sum(sqk, axis=1, keepdims=True))
        hh = num / jnp.maximum(jnp.abs(den), jnp.exp(-m_j))

        total = b_col[L - 1:L, :]
        lw_end_row = total - b_row + li_row
        lw_end_col = total - b_col + li_col
        m_new = jnp.maximum(total + m_prev, jnp.max(lw_end_row, axis=1, keepdims=True))
        a_row = jnp.exp(lw_end_row - m_new)
        a_col = jnp.exp(lw_end_col - m_new)
        decay = jnp.exp(total + m_prev - m_new)
        ka_t = (kh.astype(F32) * a_col).T.astype(BF16)
        ct_ref[h] = decay * ct_ref[h] + _dot(ka_t, vh)
        a8 = jnp.broadcast_to(a_row, (8, L)).astype(BF16)
        n_ref[h] = decay * n_ref[h] + _dot(a8, kh)
        m_ref[h:h + 1, :] = jnp.broadcast_to(m_new, (1, LANES))

        hn = hh * lax.rsqrt(jnp.mean(hh * hh, axis=-1, keepdims=True) + EPS)
        y_ref[:, sl] = (hn * ng_ref[:, sl] * jax.nn.sigmoid(o_ref[:, sl])).astype(y_ref.dtype)


def mlstm_mixer(u, qk, gate_b, norm_g, batch, seq):
    L = M_CHUNK
    nc = seq // L
    w = M_WIDTH
    rows = lambda b_, c: b_ * nc + c
    return pl.pallas_call(
        _mlstm_kernel,
        grid=(batch, nc),
        in_specs=[pl.BlockSpec((L, w), lambda b_, c: (rows(b_, c), 0)),
                  pl.BlockSpec((L, w), lambda b_, c: (rows(b_, c), 1)),
                  pl.BlockSpec((L, w), lambda b_, c: (rows(b_, c), COL_MV // w)),
                  pl.BlockSpec((L, w), lambda b_, c: (rows(b_, c), COL_MO // w)),
                  pl.BlockSpec((L, LANES), lambda b_, c: (rows(b_, c), COL_GATES // LANES)),
                  pl.BlockSpec((1, LANES), lambda b_, c: (0, 0)),
                  pl.BlockSpec((1, w), lambda b_, c: (0, 0))],
        out_specs=pl.BlockSpec((L, w), lambda b_, c: (rows(b_, c), 0)),
        out_shape=jax.ShapeDtypeStruct((batch * seq, w), BF16),
        scratch_shapes=[pltpu.VMEM((M_HEADS, M_HEAD_DIM, M_HEAD_DIM), F32),
                        pltpu.VMEM((M_HEADS, 8, M_HEAD_DIM), F32),
                        pltpu.VMEM((8, LANES), F32)],
        compiler_params=_params("arbitrary", "arbitrary"),
        name="mlstm",
    )(qk, qk, u, u, u, gate_b, norm_g.reshape(1, w))


def _ssd_kernel(xbc_ref, z_ref, g_ref, dtb_ref, alog_ref, dskip_ref, ng_ref, y_ref, s_ref):
    c = pl.program_id(1)
    L = xbc_ref.shape[0]
    GW, N, P = S_GROUP_WIDTH, S_STATE, S_HEAD_DIM
    E = S_HEADS // S_GROUPS

    @pl.when(c == 0)
    def _():
        s_ref[...] = jnp.zeros_like(s_ref)

    dt = _softplus(g_ref[...] + dtb_ref[...])
    a = dt * (-jnp.exp(alog_ref[...]))
    row = lax.broadcasted_iota(jnp.int32, (L, L), 0)
    col = lax.broadcasted_iota(jnp.int32, (L, L), 1)
    causal = row >= col
    tril = jnp.where(causal, 1.0, 0.0).astype(F32)
    acum = _dot(tril, a, HIGHEST)
    acum_t = acum.T
    lane_g = lax.broadcasted_iota(jnp.int32, (LANES, GW), 0)
    chan = lax.broadcasted_iota(jnp.int32, (LANES, GW), 1)
    lane_p = lax.broadcasted_iota(jnp.int32, (L, LANES), 1)
    for gi in range(S_GROUPS):
        expand = jnp.where(lane_g == GATE_LANE_SDT + gi * E + chan // P, 1.0, 0.0).astype(F32)
        dt_x = _dot(dt, expand, HIGHEST)
        ac_x = _dot(acum, expand, HIGHEST)
        tot_x = ac_x[L - 1:L, :]
        gs = slice(gi * GW, (gi + 1) * GW)
        x = xbc_ref[:, gs].astype(F32)
        bm = xbc_ref[:, S_WIDTH + gi * N:S_WIDTH + (gi + 1) * N]
        cm = xbc_ref[:, S_WIDTH + S_GROUPS * N + gi * N:S_WIDTH + S_GROUPS * N + (gi + 1) * N]
        xdt = x * dt_x
        cb = _dot_nt(cm, bm)
        pieces = []
        for pi in range(GW // LANES):
            xp = xdt[:, pi * LANES:(pi + 1) * LANES]
            acc = None
            for half in range(LANES // P):
                ln = GATE_LANE_SDT + gi * E + pi * (LANES // P) + half
                seg = acum[:, ln:ln + 1] - acum_t[ln:ln + 1, :]
                lmat = jnp.exp(jnp.where(causal, seg, NEG_INF))
                in_half = (lane_p >= half * P) & (lane_p < (half + 1) * P)
                d = _dot((cb * lmat).astype(BF16), jnp.where(in_half, xp, 0.0).astype(BF16))
                acc = d if acc is None else acc + d
            pieces.append(acc)
        y = jnp.concatenate(pieces, axis=1)
        state = s_ref[gi]
        y = y + _dot(cm, state.astype(BF16)) * jnp.exp(ac_x) + x * dskip_ref[:, gs]
        xdd = (xdt * jnp.exp(tot_x - ac_x)).astype(BF16)
        bm_t = bm.astype(F32).T.astype(BF16)
        s_ref[gi] = state * jnp.exp(tot_x) + _dot(bm_t, xdd)
        y = y * _silu(z_ref[:, gs])
        y_ref[:, gs] = _rms(y, ng_ref[:, gs]).astype(y_ref.dtype)


def ssd_mixer(u, xbc, dt_bias, a_log, d_skip, norm_g, batch, seq):
    L = S_CHUNK
    nc = seq // L
    rows = lambda b_, c: b_ * nc + c
    return pl.pallas_call(
        _ssd_kernel,
        grid=(batch, nc),
        in_specs=[pl.BlockSpec((L, S_XBC), lambda b_, c: (rows(b_, c), 0)),
                  pl.BlockSpec((L, S_WIDTH), lambda b_, c: (rows(b_, c), COL_SZ // S_WIDTH)),
                  pl.BlockSpec((L, LANES), lambda b_, c: (rows(b_, c), COL_GATES // LANES)),
                  pl.BlockSpec((1, LANES), lambda b_, c: (0, 0)),
                  pl.BlockSpec((1, LANES), lambda b_, c: (0, 0)),
                  pl.BlockSpec((1, S_WIDTH), lambda b_, c: (0, 0)),
                  pl.BlockSpec((1, S_WIDTH), lambda b_, c: (0, 0))],
        out_specs=pl.BlockSpec((L, S_WIDTH), lambda b_, c: (rows(b_, c), 0)),
        out_shape=jax.ShapeDtypeStruct((batch * seq, S_WIDTH), BF16),
        scratch_shapes=[pltpu.VMEM((S_GROUPS, S_STATE, S_GROUP_WIDTH), F32)],
        compiler_params=_params("arbitrary", "arbitrary"),
        name="ssd",
    )(xbc, u, u, dt_bias, a_log, d_skip.reshape(1, S_WIDTH), norm_g.reshape(1, S_WIDTH))


def _moba_kernel(q_ref, k_ref, v_ref, y_ref):
    qi = pl.program_id(2)
    BS = A_BLOCK
    S = k_ref.shape[0]
    nb = S // BS
    Dh = A_HEAD_DIM
    q = q_ref[...] * (Dh ** -0.5)
    k = k_ref[...]
    k_mean = jnp.sum(k.reshape(nb, BS, Dh), axis=1) * (1.0 / BS)
    k_mean = jnp.concatenate([k_mean, jnp.zeros((LANES - nb, Dh), F32)], axis=0)
    bs = _dot_nt(q, k_mean, HIGHEST)
    lane = lax.broadcasted_iota(jnp.int32, (BS, LANES), 1)
    cnt = jnp.zeros((BS, LANES), F32)
    for j in range(nb):
        cj = bs[:, j:j + 1]
        beats = jnp.where(cj > bs, 1.0, jnp.where((cj == bs) & (lane > j), 1.0, 0.0))
        cnt = cnt + jnp.where(qi > j, beats, 0.0)
    sel = jnp.where((lane < qi) & (cnt < A_TOPK), 1.0, 0.0)
    blk_row = lax.broadcasted_iota(jnp.int32, (LANES, S), 0)
    key_blk = lax.broadcasted_iota(jnp.int32, (LANES, S), 1) // BS
    expand = jnp.where(blk_row == key_blk, 1.0, 0.0).astype(BF16)
    sel_x = _dot(sel.astype(BF16), expand)
    key = lax.broadcasted_iota(jnp.int32, (BS, S), 1)
    qpos = lax.broadcasted_iota(jnp.int32, (BS, S), 0) + qi * BS
    allowed = jnp.where(key // BS == qi, jnp.where(key <= qpos, 1.0, 0.0), sel_x)
    s = jnp.where(allowed > 0.5, _dot_nt(q.astype(BF16), k.astype(BF16)), NEG_INF)
    m = jnp.max(s, axis=1, keepdims=True)
    p = jnp.exp(s - m)
    l = jnp.sum(p, axis=1, keepdims=True)
    y_ref[...] = (_dot(p.astype(BF16), v_ref[...].astype(BF16)) / l).astype(y_ref.dtype)


def moba_mixer(u, batch, seq):
    nq = seq // A_BLOCK
    cq, ck, cv = COL_AQ // LANES, COL_AK // LANES, COL_AV // LANES
    return pl.pallas_call(
        _moba_kernel,
        grid=(batch, A_HEADS, nq),
        in_specs=[pl.BlockSpec((A_BLOCK, A_HEAD_DIM), lambda b_, h, i: (b_ * nq + i, cq + h)),
                  pl.BlockSpec((seq, A_HEAD_DIM), lambda b_, h, i: (b_, ck + h)),
                  pl.BlockSpec((seq, A_HEAD_DIM), lambda b_, h, i: (b_, cv + h))],
        out_specs=pl.BlockSpec((A_BLOCK, A_HEAD_DIM), lambda b_, h, i: (b_ * nq + i, h)),
        out_shape=jax.ShapeDtypeStruct((batch * seq, A_WIDTH), BF16),
        compiler_params=_params("arbitrary", "arbitrary", "arbitrary"),
        name="moba",
    )(u, u, u)


def _out_proj_body(ym_ref, ys_ref, ya_ref, x_ref, w_ref, g_ref):
    x = x_ref[...]
    x = x + _dot(ym_ref[...], w_ref[0:M_WIDTH, :])
    x = x + _dot(ys_ref[...], w_ref[M_WIDTH:M_WIDTH + S_WIDTH, :])
    x = x + _dot(ya_ref[...], w_ref[M_WIDTH + S_WIDTH:, :])
    return x, _rms(x, g_ref[...])


def _out_proj_kernel(ym_ref, ys_ref, ya_ref, x_ref, w_ref, g_ref, xo_ref, hn_ref):
    x, hn = _out_proj_body(ym_ref, ys_ref, ya_ref, x_ref, w_ref, g_ref)
    xo_ref[...] = x
    hn_ref[...] = hn.astype(hn_ref.dtype)


def _out_proj_route_kernel(ym_ref, ys_ref, ya_ref, x_ref, w_ref, g_ref, wr_ref, xo_ref, hn_ref, route_ref):
    x, hn = _out_proj_body(ym_ref, ys_ref, ya_ref, x_ref, w_ref, g_ref)
    xo_ref[...] = x
    hn_ref[...] = hn
    logits = _dot(hn, wr_ref[...], HIGHEST)
    lane = lax.broadcasted_iota(jnp.int32, logits.shape, 1)
    logits = jnp.where(lane < N_EXPERTS, logits, NEG_INF)
    m1 = jnp.max(logits, axis=1, keepdims=True)
    i1 = jnp.min(jnp.where(logits == m1, lane, LANES), axis=1, keepdims=True)
    rest = jnp.where(lane == i1, NEG_INF, logits)
    m2 = jnp.max(rest, axis=1, keepdims=True)
    i2 = jnp.min(jnp.where(rest == m2, lane, LANES), axis=1, keepdims=True)
    e2 = jnp.exp(m2 - m1)
    w1 = 1.0 / (1.0 + e2)
    w2 = e2 / (1.0 + e2)
    route_ref[...] = jnp.where(lane == 0, i1.astype(F32),
                               jnp.where(lane == 1, i2.astype(F32),
                                         jnp.where(lane == 2, w1, jnp.where(lane == 3, w2, 0.0))))


def out_proj(ym, ys, ya, x, w, g, w_router=None):
    t, d = x.shape
    tm = ROW_TILE
    row = lambda i: (i, 0)
    fixed = lambda i: (0, 0)
    in_specs = [pl.BlockSpec((tm, M_WIDTH), row), pl.BlockSpec((tm, S_WIDTH), row),
                pl.BlockSpec((tm, A_WIDTH), row), pl.BlockSpec((tm, d), row),
                pl.BlockSpec((d, d), fixed), pl.BlockSpec((1, d), fixed)]
    args = [ym, ys, ya, x, w, g.reshape(1, d)]
    if w_router is None:
        kern = _out_proj_kernel
        out_specs = [pl.BlockSpec((tm, d), row), pl.BlockSpec((tm, d), row)]
        out_shape = [jax.ShapeDtypeStruct((t, d), F32), jax.ShapeDtypeStruct((t, d), BF16)]
    else:
        kern = _out_proj_route_kernel
        in_specs.append(pl.BlockSpec((d, LANES), fixed))
        args.append(w_router)
        out_specs = [pl.BlockSpec((tm, d), row), pl.BlockSpec((tm, d), row),
                     pl.BlockSpec((tm, LANES), row)]
        out_shape = [jax.ShapeDtypeStruct((t, d), F32), jax.ShapeDtypeStruct((t, d), F32),
                     jax.ShapeDtypeStruct((t, LANES), F32)]
    return pl.pallas_call(
        kern, grid=(t // tm,), in_specs=in_specs, out_specs=out_specs, out_shape=out_shape,
        compiler_params=_params("arbitrary"), name="out_proj",
    )(*args)


def _ffn_kernel(hn_ref, wg_ref, wu_ref, wd_ref, x_ref, g_ref, xo_ref, hp_ref, acc_ref):
    f = pl.program_id(1)

    @pl.when(f == 0)
    def _():
        acc_ref[...] = jnp.zeros_like(acc_ref)

    hn = hn_ref[...]
    h = _silu(_dot(hn, wg_ref[...])) * _dot(hn, wu_ref[...])
    acc_ref[...] += _dot(h.astype(BF16), wd_ref[...])

    @pl.when(f == pl.num_programs(1) - 1)
    def _():
        x = x_ref[...] + acc_ref[...]
        xo_ref[...] = x
        hp_ref[...] = _rms(x, g_ref[...]).astype(hp_ref.dtype)


def dense_ffn(hn, wg, wu, wd, x, g_next):
    t, d = x.shape
    ff = wg.shape[1]
    tm, tf = ROW_TILE, FF_TILE
    return pl.pallas_call(
        _ffn_kernel,
        grid=(t // tm, ff // tf),
        in_specs=[pl.BlockSpec((tm, d), lambda i, f: (i, 0)),
                  pl.BlockSpec((d, tf), lambda i, f: (0, f)),
                  pl.BlockSpec((d, tf), lambda i, f: (0, f)),
                  pl.BlockSpec((tf, d), lambda i, f: (f, 0)),
                  pl.BlockSpec((tm, d), lambda i, f: (i, 0)),
                  pl.BlockSpec((1, d), lambda i, f: (0, 0))],
        out_specs=[pl.BlockSpec((tm, d), lambda i, f: (i, 0)),
                   pl.BlockSpec((tm, d), lambda i, f: (i, 0))],
        out_shape=[jax.ShapeDtypeStruct((t, d), F32), jax.ShapeDtypeStruct((t, d), BF16)],
        scratch_shapes=[pltpu.VMEM((tm, d), F32)],
        compiler_params=_params("arbitrary", "arbitrary"),
        name="dense_ffn",
    )(hn, wg, wu, wd, x, g_next.reshape(1, d))


def _gather_rows_kernel(src_ref, h_hbm, o_ref, buf_ref, sem):
    i = pl.program_id(0)
    tg = o_ref.shape[0]
    base = i * tg

    def copy(r):
        return pltpu.make_async_copy(h_hbm.at[pl.ds(src_ref[base + r], 1)],
                                     buf_ref.at[pl.ds(r, 1)], sem)

    def start(r, carry):
        copy(r).start()
        return carry

    def wait(r, carry):
        copy(r).wait()
        return carry

    lax.fori_loop(0, tg, start, 0)
    lax.fori_loop(0, tg, wait, 0)
    o_ref[...] = buf_ref[...].astype(o_ref.dtype)


def gather_rows(h, src):
    t, d = h.shape
    r = src.shape[0]
    tg = GATHER_TILE
    return pl.pallas_call(
        _gather_rows_kernel,
        grid_spec=pltpu.PrefetchScalarGridSpec(
            num_scalar_prefetch=1,
            grid=(r // tg,),
            in_specs=[pl.BlockSpec(memory_space=pl.ANY)],
            out_specs=pl.BlockSpec((tg, d), lambda i, s: (i, 0)),
            scratch_shapes=[pltpu.VMEM((tg, d), F32), pltpu.SemaphoreType.DMA(())]),
        out_shape=jax.ShapeDtypeStruct((r, d), BF16),
        compiler_params=_params("arbitrary"),
        name="moe_dispatch",
    )(src, h)


def _expert_kernel(te_ref, nu_ref, xs_ref, wg_ref, wu_ref, wd_ref, y_ref, acc_ref):
    i = pl.program_id(0)
    f = pl.program_id(1)
    used = i < nu_ref[0]

    @pl.when(f == 0)
    def _():
        acc_ref[...] = jnp.zeros_like(acc_ref)

    @pl.when(used)
    def _():
        xs = xs_ref[...]
        h = _silu(_dot(xs, wg_ref[0])) * _dot(xs, wu_ref[0])
        acc_ref[...] += _dot(h.astype(BF16), wd_ref[0])

    @pl.when(f == pl.num_programs(1) - 1)
    def _():
        y_ref[...] = acc_ref[...]


def expert_ffn(xs, wg, wu, wd, tile_expert, n_used):
    r, d = xs.shape
    ff = wg.shape[2]
    tm, tf = MOE_ROW_TILE, FF_TILE
    nf = ff // tf

    def fblk(i, f, nu):
        return jnp.where(i < nu[0], f, nf - 1)

    return pl.pallas_call(
        _expert_kernel,
        grid_spec=pltpu.PrefetchScalarGridSpec(
            num_scalar_prefetch=2,
            grid=(r // tm, nf),
            in_specs=[pl.BlockSpec((tm, d), lambda i, f, te, nu: (i, 0)),
                      pl.BlockSpec((1, d, tf), lambda i, f, te, nu: (te[i], 0, fblk(i, f, nu))),
                      pl.BlockSpec((1, d, tf), lambda i, f, te, nu: (te[i], 0, fblk(i, f, nu))),
                      pl.BlockSpec((1, tf, d), lambda i, f, te, nu: (te[i], fblk(i, f, nu), 0))],
            out_specs=pl.BlockSpec((tm, d), lambda i, f, te, nu: (i, 0)),
            scratch_shapes=[pltpu.VMEM((tm, d), F32)]),
        out_shape=jax.ShapeDtypeStruct((r, d), F32),
        compiler_params=_params("arbitrary", "arbitrary"),
        name="moe_experts",
    )(tile_expert, n_used, xs, wg, wu, wd)


def _combine_kernel(pos_ref, y_hbm, x_ref, route_ref, g_ref, xo_ref, hp_ref, buf_ref, sem):
    i = pl.program_id(0)
    tc = x_ref.shape[0]
    t_total = pl.num_programs(0) * tc
    base = i * tc

    def copy(r, slot):
        return pltpu.make_async_copy(y_hbm.at[pl.ds(pos_ref[slot * t_total + base + r], 1)],
                                     buf_ref.at[slot, pl.ds(r, 1)], sem)

    def start(r, carry):
        copy(r, 0).start()
        copy(r, 1).start()
        return carry

    def wait(r, carry):
        copy(r, 0).wait()
        copy(r, 1).wait()
        return carry

    lax.fori_loop(0, tc, start, 0)
    lax.fori_loop(0, tc, wait, 0)
    route = route_ref[...]
    x = x_ref[...] + route[:, 2:3] * buf_ref[0] + route[:, 3:4] * buf_ref[1]
    xo_ref[...] = x
    hp_ref[...] = _rms(x, g_ref[...]).astype(hp_ref.dtype)


def moe_combine(y, pos, x, route, g_next):
    t, d = x.shape
    tc = GATHER_TILE
    return pl.pallas_call(
        _combine_kernel,
        grid_spec=pltpu.PrefetchScalarGridSpec(
            num_scalar_prefetch=1,
            grid=(t // tc,),
            in_specs=[pl.BlockSpec(memory_space=pl.ANY),
                      pl.BlockSpec((tc, d), lambda i, p_: (i, 0)),
                      pl.BlockSpec((tc, LANES), lambda i, p_: (i, 0)),
                      pl.BlockSpec((1, d), lambda i, p_: (0, 0))],
            out_specs=[pl.BlockSpec((tc, d), lambda i, p_: (i, 0)),
                       pl.BlockSpec((tc, d), lambda i, p_: (i, 0))],
            scratch_shapes=[pltpu.VMEM((2, tc, d), F32), pltpu.SemaphoreType.DMA(())]),
        out_shape=[jax.ShapeDtypeStruct((t, d), F32), jax.ShapeDtypeStruct((t, d), BF16)],
        compiler_params=_params("arbitrary"),
        name="moe_combine",
    )(pos, y, x, route, g_next.reshape(1, d))


def moe_ffn(hn, route, x, wg, wu, wd, g_next):
    t, d = x.shape
    tm = MOE_ROW_TILE
    n_tiles = (2 * t) // tm + N_EXPERTS
    e_flat = jnp.concatenate([route[:, 0], route[:, 1]]).astype(jnp.int32)
    onehot = (e_flat[:, None] == jnp.arange(N_EXPERTS, dtype=jnp.int32)[None, :]).astype(jnp.int32)
    counts = jnp.sum(onehot, axis=0)
    rank = jnp.sum((jnp.cumsum(onehot, axis=0) - onehot) * onehot, axis=1)
    padded = ((counts + tm - 1) // tm) * tm
    ends = jnp.cumsum(padded)
    offs = ends - padded
    pos = (jnp.sum(onehot * offs[None, :], axis=1) + rank).astype(jnp.int32)
    token = jnp.concatenate([jnp.arange(t, dtype=jnp.int32)] * 2)
    src = jnp.zeros((n_tiles * tm,), jnp.int32).at[pos].set(token)
    n_used = (ends[-1] // tm).astype(jnp.int32).reshape(1)
    tile_start = jnp.minimum(jnp.arange(n_tiles, dtype=jnp.int32), n_used[0] - 1) * tm
    tile_expert = jnp.sum((ends[None, :] <= tile_start[:, None]).astype(jnp.int32), axis=1)
    tile_expert = jnp.minimum(tile_expert, N_EXPERTS - 1).astype(jnp.int32)

    xs = gather_rows(hn, src)
    y = expert_ffn(xs, wg, wu, wd, tile_expert, n_used)
    return moe_combine(y, pos, x, route, g_next)


def _ple_kernel(hp_ref, p_ref, wg_ref, wp_ref, x_ref, g_ref, xo_ref, xn_ref):
    gate = jax.nn.sigmoid(_dot(hp_ref[...], wg_ref[...]))
    x = x_ref[...] + _dot(p_ref[...].astype(BF16), wp_ref[...]) * gate
    xo_ref[...] = x
    xn_ref[...] = _rms(x, g_ref[...]).astype(xn_ref.dtype)


def _ple_final_kernel(hp_ref, p_ref, wg_ref, wp_ref, x_ref, g_ref, o_ref):
    gate = jax.nn.sigmoid(_dot(hp_ref[...], wg_ref[...]))
    x = x_ref[...] + _dot(p_ref[...].astype(BF16), wp_ref[...]) * gate
    o_ref[...] = _rms(x, g_ref[...])


def ple(hp, p, wg, wp, x, g_next, final):
    t, d = x.shape
    dp = p.shape[1]
    tm = ROW_TILE
    row = lambda i: (i, 0)
    fixed = lambda i: (0, 0)
    in_specs = [pl.BlockSpec((tm, d), row), pl.BlockSpec((tm, dp), row),
                pl.BlockSpec((d, d), fixed), pl.BlockSpec((dp, d), fixed),
                pl.BlockSpec((tm, d), row), pl.BlockSpec((1, d), fixed)]
    if final:
        kern = _ple_final_kernel
        out_specs = pl.BlockSpec((tm, d), row)
        out_shape = jax.ShapeDtypeStruct((t, d), F32)
    else:
        kern = _ple_kernel
        out_specs = [pl.BlockSpec((tm, d), row), pl.BlockSpec((tm, d), row)]
        out_shape = [jax.ShapeDtypeStruct((t, d), F32), jax.ShapeDtypeStruct((t, d), BF16)]
    return pl.pallas_call(
        kern, grid=(t // tm,), in_specs=in_specs, out_specs=out_specs, out_shape=out_shape,
        compiler_params=_params("arbitrary"), name="ple",
    )(hp, p, wg, wp, x, g_next.reshape(1, d))


def _pack_w_in(w):
    sizes = (M_WIDTH, M_WIDTH, M_WIDTH, M_WIDTH, M_HEADS, M_HEADS, S_WIDTH, S_XBC, S_HEADS,
             A_WIDTH, A_WIDTH, A_WIDTH)
    offs = [0]
    for s in sizes:
        offs.append(offs[-1] + s)
    seg = [w[:, offs[i]:offs[i + 1]] for i in range(len(sizes))]
    mq, mk, mv, mo, mi, mf, sz, sxbc, sdt, aq, ak, av = seg
    pad = jnp.zeros((w.shape[0], LANES - 2 * M_HEADS - S_HEADS), w.dtype)
    return jnp.concatenate([mq, mk, mv, mo, sz, sxbc, aq, ak, av, mi, mf, sdt, pad], axis=1).astype(BF16)


def _gate_lanes(vec, lane0):
    return jnp.zeros((1, LANES), F32).at[0, lane0:lane0 + vec.shape[0]].set(vec.astype(F32))


def kernel(x, p, ln_mix, w_in, w_out, m_conv_w, m_conv_b, m_gate_b, m_norm_g, s_conv_w, s_conv_b, s_dt_bias, s_a_log, s_d, s_norm_g, ln_ffn, ffn_w_gate, ffn_w_up, ffn_w_down, moe_router, moe_w_gate, moe_w_up, moe_w_down, ln_ple, ple_proj, ple_gate, ln_final):
    batch, seq, d = x.shape
    depth = w_in.shape[0]
    t = batch * seq
    xf = x.reshape(t, d)
    qk_scale = jnp.concatenate([jnp.ones((M_WIDTH,), F32),
                                jnp.full((M_WIDTH,), M_HEAD_DIM ** -0.5, F32)])
    xbc_scale = jnp.ones((S_XBC,), F32)
    xn = rmsnorm_bf16(xf, ln_mix[0])
    out = None
    for i in range(depth):
        u = in_proj(xn, _pack_w_in(w_in[i]))
        qk = conv_silu(u, COL_MQ, 2 * M_WIDTH, m_conv_w[i], m_conv_b[i], qk_scale, batch, seq)
        xbc = conv_silu(u, COL_SXBC, S_XBC, s_conv_w[i], s_conv_b[i], xbc_scale, batch, seq)
        y_m = mlstm_mixer(u, qk, _gate_lanes(m_gate_b[i], GATE_LANE_MI), m_norm_g[i], batch, seq)
        y_s = ssd_mixer(u, xbc, _gate_lanes(s_dt_bias[i], GATE_LANE_SDT),
                        _gate_lanes(s_a_log[i], GATE_LANE_SDT),
                        jnp.repeat(s_d[i].astype(F32), S_HEAD_DIM), s_norm_g[i], batch, seq)
        y_a = moba_mixer(u, batch, seq)
        w_o = w_out[i].astype(BF16)
        j = i // 2
        if i % 2 == 0:
            xf, hn = out_proj(y_m, y_s, y_a, xf, w_o, ln_ffn[i])
            xf, hp = dense_ffn(hn, ffn_w_gate[j].astype(BF16), ffn_w_up[j].astype(BF16),
                               ffn_w_down[j].astype(BF16), xf, ln_ple[i])
        else:
            w_r = jnp.zeros((d, LANES), F32).at[:, :N_EXPERTS].set(moe_router[j].astype(F32))
            xf, hn, route = out_proj(y_m, y_s, y_a, xf, w_o, ln_ffn[i], w_r)
            xf, hp = moe_ffn(hn, route, xf, moe_w_gate[j].astype(BF16), moe_w_up[j].astype(BF16),
                             moe_w_down[j].astype(BF16), ln_ple[i])
        final = i == depth - 1
        g_next = ln_final if final else ln_mix[i + 1]
        res = ple(hp, p[i].reshape(t, -1), ple_gate[i].astype(BF16), ple_proj[i].astype(BF16),
                  xf, g_next, final)
        if final:
            out = res
        else:
            xf, xn = res
    return out.reshape(batch, seq, d)
```

```python
import functools

import jax
import jax.numpy as jnp
from jax import lax
from jax.experimental import pallas as pl
from jax.experimental.pallas import tpu as pltpu

F32 = jnp.float32
BF16 = jnp.bfloat16
HIGHEST = lax.Precision.HIGHEST
NEG_INF = float("-inf")

EPS = 1e-6
LANES = 128
VMEM_LIMIT = 56 * 1024 * 1024

D_MODEL = 2048
M_HEADS = 4
M_HEAD_DIM = 128
M_WIDTH = M_HEADS * M_HEAD_DIM
S_HEADS = 16
S_HEAD_DIM = 64
S_WIDTH = S_HEADS * S_HEAD_DIM
S_GROUPS = 2
S_STATE = 128
S_XBC = S_WIDTH + 2 * S_GROUPS * S_STATE
S_GROUP_WIDTH = S_WIDTH // S_GROUPS
A_HEADS = 4
A_HEAD_DIM = 128
A_WIDTH = A_HEADS * A_HEAD_DIM
A_BLOCK = 256
A_TOPK = 3
N_EXPERTS = 8
CONV_K = 4

COL_MQ = 0
COL_MK = COL_MQ + M_WIDTH
COL_MV = COL_MK + M_WIDTH
COL_MO = COL_MV + M_WIDTH
COL_SZ = COL_MO + M_WIDTH
COL_SXBC = COL_SZ + S_WIDTH
COL_AQ = COL_SXBC + S_XBC
COL_AK = COL_AQ + A_WIDTH
COL_AV = COL_AK + A_WIDTH
COL_GATES = COL_AV + A_WIDTH
D_PROJ_PACKED = COL_GATES + LANES
GATE_LANE_MI = 0
GATE_LANE_MF = M_HEADS
GATE_LANE_SDT = 2 * M_HEADS

ROW_TILE = 512
PROJ_ROW_TILE = 1024
PROJ_COL_TILE = 896
FF_TILE = 512
MOE_FF_TILE = 1024
M_CHUNK = 256
S_CHUNK = 128
MOE_ROW_TILE = 512
GATHER_TILE = 256
DMA_UNROLL = 8


def _params(*sem):
    return pltpu.CompilerParams(dimension_semantics=sem, vmem_limit_bytes=VMEM_LIMIT)


def _rms(x, g):
    return x * lax.rsqrt(jnp.mean(x * x, axis=-1, keepdims=True) + EPS) * g


def _dot(a, b, precision=None):
    return jnp.dot(a, b, preferred_element_type=F32, precision=precision)


def _dot_nt(a, b, precision=None):
    return lax.dot_general(a, b, (((1,), (1,)), ((), ())), preferred_element_type=F32,
                           precision=precision)


def _softplus(x):
    return jnp.maximum(x, 0.0) + jnp.log1p(jnp.exp(-jnp.abs(x)))


def _silu(x):
    return x * jax.nn.sigmoid(x)


def _rmsnorm_kernel(x_ref, g_ref, o_ref):
    o_ref[...] = _rms(x_ref[...], g_ref[...]).astype(o_ref.dtype)


def rmsnorm_bf16(x, g):
    t, d = x.shape
    return pl.pallas_call(
        _rmsnorm_kernel,
        grid=(t // ROW_TILE,),
        in_specs=[pl.BlockSpec((ROW_TILE, d), lambda i: (i, 0)),
                  pl.BlockSpec((1, d), lambda i: (0, 0))],
        out_specs=pl.BlockSpec((ROW_TILE, d), lambda i: (i, 0)),
        out_shape=jax.ShapeDtypeStruct((t, d), BF16),
        compiler_params=_params("arbitrary"),
        name="rmsnorm",
    )(x, g.reshape(1, d))


def _proj_kernel(x_ref, w_ref, o_ref):
    o_ref[...] = _dot(x_ref[...], w_ref[...])


def in_proj(xn, w):
    t, d = xn.shape
    n = w.shape[1]
    tm = PROJ_ROW_TILE
    return pl.pallas_call(
        _proj_kernel,
        grid=(n // PROJ_COL_TILE, t // tm),
        in_specs=[pl.BlockSpec((tm, d), lambda j, i: (i, 0)),
                  pl.BlockSpec((d, PROJ_COL_TILE), lambda j, i: (0, j))],
        out_specs=pl.BlockSpec((tm, PROJ_COL_TILE), lambda j, i: (i, j)),
        out_shape=jax.ShapeDtypeStruct((t, n), F32),
        compiler_params=_params("arbitrary", "arbitrary"),
        name="in_proj",
    )(xn, w)


def _conv_silu_kernel(u_ref, w_ref, b_ref, s_ref, o_ref):
    x = u_ref[...]
    row = lax.broadcasted_iota(jnp.int32, x.shape, 0)
    y = x * w_ref[CONV_K - 1:CONV_K, :] + b_ref[...]
    for shift in range(1, CONV_K):
        xs = jnp.where(row >= shift, pltpu.roll(x, shift, axis=0), 0.0)
        y = y + xs * w_ref[CONV_K - 1 - shift:CONV_K - shift, :]
    o_ref[...] = (_silu(y) * s_ref[...]).astype(o_ref.dtype)


def conv_silu(u, col0, width, w, b, post_scale, batch, seq):
    ct = 512
    cb0 = col0 // ct
    return pl.pallas_call(
        _conv_silu_kernel,
        grid=(batch, width // ct),
        in_specs=[pl.BlockSpec((seq, ct), lambda b_, c: (b_, cb0 + c)),
                  pl.BlockSpec((CONV_K, ct), lambda b_, c: (0, c)),
                  pl.BlockSpec((1, ct), lambda b_, c: (0, c)),
                  pl.BlockSpec((1, ct), lambda b_, c: (0, c))],
        out_specs=pl.BlockSpec((seq, ct), lambda b_, c: (b_, c)),
        out_shape=jax.ShapeDtypeStruct((batch * seq, width), BF16),
        compiler_params=_params("arbitrary", "arbitrary"),
        name="conv_silu",
    )(u, w, b.reshape(1, width), post_scale.reshape(1, width))


def _mlstm_kernel(q_ref, k_ref, v_ref, o_ref, g_ref, gb_ref, ng_ref, y_ref, ct_ref, n_ref, m_ref):
    c = pl.program_id(1)
    L = q_ref.shape[0]
    H, Dh = M_HEADS, M_HEAD_DIM

    @pl.when(c == 0)
    def _():
        ct_ref[...] = jnp.zeros_like(ct_ref)
        n_ref[...] = jnp.zeros_like(n_ref)
        m_ref[...] = jnp.zeros_like(m_ref)

    g = g_ref[...] + gb_ref[...]
    lf = jnp.minimum(g, 0.0) - jnp.log1p(jnp.exp(-jnp.abs(g)))
    row = lax.broadcasted_iota(jnp.int32, (L, L), 0)
    col = lax.broadcasted_iota(jnp.int32, (L, L), 1)
    causal = row >= col
    tril = jnp.where(causal, 1.0, 0.0).astype(F32)
    bcum = _dot(tril, lf, HIGHEST)
    g_t = g.T
    bcum_t = bcum.T
    for h in range(H):
        li_col = g[:, GATE_LANE_MI + h:GATE_LANE_MI + h + 1]
        li_row = g_t[GATE_LANE_MI + h:GATE_LANE_MI + h + 1, :]
        b_col = bcum[:, GATE_LANE_MF + h:GATE_LANE_MF + h + 1]
        b_row = bcum_t[GATE_LANE_MF + h:GATE_LANE_MF + h + 1, :]
        m_prev = m_ref[h:h + 1, 0:1]
        sl = slice(h * Dh, (h + 1) * Dh)
        qh = q_ref[:, sl]
        kh = k_ref[:, sl]
        vh = v_ref[:, sl].astype(BF16)

        logw = jnp.where(causal, b_col - b_row + li_row, NEG_INF)
        m_inter = b_col + m_prev
        m_j = jnp.maximum(m_inter, jnp.max(logw, axis=1, keepdims=True))
        w = jnp.exp(logw - m_j)
        s_inter = jnp.exp(m_inter - m_j)
        sqk = _dot_nt(qh, kh) * w
        num = s_inter * _dot(qh, ct_ref[h].astype(BF16)) + _dot(sqk.astype(BF16), vh)
        n_row = n_ref[h][0:1, :]
        den = (s_inter * jnp.sum(qh.astype(F32) * n_row, axis=1, keepdims=True)
               + jnp.sum(sqk, axis=1, keepdims=True))
        hh = num / jnp.maximum(jnp.abs(den), jnp.exp(-m_j))

        total = b_col[L - 1:L, :]
        lw_end_row = total - b_row + li_row
        lw_end_col = total - b_col + li_col
        m_new = jnp.maximum(total + m_prev, jnp.max(lw_end_row, axis=1, keepdims=True))
        a_row = jnp.exp(lw_end_row - m_new)
        a_col = jnp.exp(lw_end_col - m_new)
        decay = jnp.exp(total + m_prev - m_new)
        ka_t = (kh.astype(F32) * a_col).T.astype(BF16)
        ct_ref[h] = decay * ct_ref[h] + _dot(ka_t, vh)
        a8 = jnp.broadcast_to(a_row, (8, L)).astype(BF16)
        n_ref[h] = decay * n_ref[h] + _dot(a8, kh)
        m_ref[h:h + 1, :] = jnp.broadcast_to(m_new, (1, LANES))

        hn = hh * lax.rsqrt(jnp.mean(hh * hh, axis=-1, keepdims=True) + EPS)
        y_ref[:, sl] = (hn * ng_ref[:, sl] * jax.nn.sigmoid(o_ref[:, sl])).astype(y_ref.dtype)


def mlstm_mixer(u, qk, gate_b, norm_g, batch, seq):
    L = M_CHUNK
    nc = seq // L
    w = M_WIDTH
    rows = lambda b_, c: b_ * nc + c
    return pl.pallas_call(
        _mlstm_kernel,
        grid=(batch, nc),
        in_specs=[pl.BlockSpec((L, w), lambda b_, c: (rows(b_, c), 0)),
                  pl.BlockSpec((L, w), lambda b_, c: (rows(b_, c), 1)),
                  pl.BlockSpec((L, w), lambda b_, c: (rows(b_, c), COL_MV // w)),
                  pl.BlockSpec((L, w), lambda b_, c: (rows(b_, c), COL_MO // w)),
                  pl.BlockSpec((L, LANES), lambda b_, c: (rows(b_, c), COL_GATES // LANES)),
                  pl.BlockSpec((1, LANES), lambda b_, c: (0, 0)),
                  pl.BlockSpec((1, w), lambda b_, c: (0, 0))],
        out_specs=pl.BlockSpec((L, w), lambda b_, c: (rows(b_, c), 0)),
        out_shape=jax.ShapeDtypeStruct((batch * seq, w), BF16),
        scratch_shapes=[pltpu.VMEM((M_HEADS, M_HEAD_DIM, M_HEAD_DIM), F32),
                        pltpu.VMEM((M_HEADS, 8, M_HEAD_DIM), F32),
                        pltpu.VMEM((8, LANES), F32)],
        compiler_params=_params("arbitrary", "arbitrary"),
        name="mlstm",
    )(qk, qk, u, u, u, gate_b, norm_g.reshape(1, w))


def _ssd_kernel(xbc_ref, z_ref, g_ref, dtb_ref, alog_ref, dskip_ref, ng_ref, y_ref, s_ref):
    c = pl.program_id(1)
    L = xbc_ref.shape[0]
    GW, N, P = S_GROUP_WIDTH, S_STATE, S_HEAD_DIM
    E = S_HEADS // S_GROUPS

    @pl.when(c == 0)
    def _():
        s_ref[...] = jnp.zeros_like(s_ref)

    dt = _softplus(g_ref[...] + dtb_ref[...])
    a = dt * (-jnp.exp(alog_ref[...]))
    row = lax.broadcasted_iota(jnp.int32, (L, L), 0)
    col = lax.broadcasted_iota(jnp.int32, (L, L), 1)
    causal = row >= col
    tril = jnp.where(causal, 1.0, 0.0).astype(F32)
    acum = _dot(tril, a, HIGHEST)
    acum_t = acum.T
    lane_p = lax.broadcasted_iota(jnp.int32, (L, LANES), 1)
    heads_per_slab = LANES // P
    for gi in range(S_GROUPS):
        gs = slice(gi * GW, (gi + 1) * GW)
        x = xbc_ref[:, gs].astype(F32)
        bm = xbc_ref[:, S_WIDTH + gi * N:S_WIDTH + (gi + 1) * N]
        cm = xbc_ref[:, S_WIDTH + S_GROUPS * N + gi * N:S_WIDTH + S_GROUPS * N + (gi + 1) * N]
        cb = _dot_nt(cm, bm)
        pieces, xdt_slabs, ac_slabs = [], [], []
        for pi in range(GW // LANES):
            x_slab = x[:, pi * LANES:(pi + 1) * LANES]
            dt_slab = ac_slab = None
            per_head = []
            for half in range(heads_per_slab):
                ln = GATE_LANE_SDT + gi * E + pi * heads_per_slab + half
                ac_col = jnp.broadcast_to(acum[:, ln:ln + 1], (L, LANES))
                dt_col = jnp.broadcast_to(dt[:, ln:ln + 1], (L, LANES))
                in_half = lane_p >= half * P
                dt_slab = dt_col if dt_slab is None else jnp.where(in_half, dt_col, dt_slab)
                ac_slab = ac_col if ac_slab is None else jnp.where(in_half, ac_col, ac_slab)
                seg = jnp.broadcast_to(acum[:, ln:ln + 1], (L, L)) - acum_t[ln:ln + 1, :]
                per_head.append(jnp.exp(jnp.where(causal, seg, NEG_INF)))
            xdt_slab = x_slab * dt_slab
            acc = None
            for half in range(heads_per_slab):
                in_half = (lane_p >= half * P) & (lane_p < (half + 1) * P)
                d = _dot((cb * per_head[half]).astype(BF16),
                         jnp.where(in_half, xdt_slab, 0.0).astype(BF16))
                acc = d if acc is None else acc + d
            pieces.append(acc)
            xdt_slabs.append(xdt_slab)
            ac_slabs.append(ac_slab)
        y = jnp.concatenate(pieces, axis=1)
        xdt = jnp.concatenate(xdt_slabs, axis=1)
        ac_x = jnp.concatenate(ac_slabs, axis=1)
        tot_x = ac_x[L - 1:L, :]
        state = s_ref[gi]
        y = y + _dot(cm, state.astype(BF16)) * jnp.exp(ac_x) + x * dskip_ref[:, gs]
        xdd = (xdt * jnp.exp(tot_x - ac_x)).astype(BF16)
        bm_t = bm.astype(F32).T.astype(BF16)
        s_ref[gi] = state * jnp.exp(tot_x) + _dot(bm_t, xdd)
        y = y * _silu(z_ref[:, gs])
        y_ref[:, gs] = _rms(y, ng_ref[:, gs]).astype(y_ref.dtype)


def ssd_mixer(u, xbc, dt_bias, a_log, d_skip, norm_g, batch, seq):
    L = S_CHUNK
    nc = seq // L
    rows = lambda b_, c: b_ * nc + c
    return pl.pallas_call(
        _ssd_kernel,
        grid=(batch, nc),
        in_specs=[pl.BlockSpec((L, S_XBC), lambda b_, c: (rows(b_, c), 0)),
                  pl.BlockSpec((L, S_WIDTH), lambda b_, c: (rows(b_, c), COL_SZ // S_WIDTH)),
                  pl.BlockSpec((L, LANES), lambda b_, c: (rows(b_, c), COL_GATES // LANES)),
                  pl.BlockSpec((1, LANES), lambda b_, c: (0, 0)),
                  pl.BlockSpec((1, LANES), lambda b_, c: (0, 0)),
                  pl.BlockSpec((1, S_WIDTH), lambda b_, c: (0, 0)),
                  pl.BlockSpec((1, S_WIDTH), lambda b_, c: (0, 0))],
        out_specs=pl.BlockSpec((L, S_WIDTH), lambda b_, c: (rows(b_, c), 0)),
        out_shape=jax.ShapeDtypeStruct((batch * seq, S_WIDTH), BF16),
        scratch_shapes=[pltpu.VMEM((S_GROUPS, S_STATE, S_GROUP_WIDTH), F32)],
        compiler_params=_params("arbitrary", "arbitrary"),
        name="ssd",
    )(xbc, u, u, dt_bias, a_log, d_skip.reshape(1, S_WIDTH), norm_g.reshape(1, S_WIDTH))


def _moba_kernel(q_ref, k_ref, v_ref, ex_ref, y_ref, kb_ref, vb_ref, kmean_ref):
    qi = pl.program_id(2)
    BS = A_BLOCK
    S = k_ref.shape[0]
    nb = S // BS
    nbp = kmean_ref.shape[0]
    Dh = A_HEAD_DIM

    @pl.when(qi == 0)
    def _():
        k = k_ref[...]
        kb_ref[...] = k.astype(BF16)
        vb_ref[...] = v_ref[...].astype(BF16)
        k_mean = jnp.sum(k.reshape(nb, BS, Dh), axis=1) * (1.0 / BS)
        if nbp > nb:
            k_mean = jnp.concatenate([k_mean, jnp.zeros((nbp - nb, Dh), F32)], axis=0)
        kmean_ref[...] = k_mean

    q = q_ref[...] * (Dh ** -0.5)
    bs_t = _dot_nt(kmean_ref[...], q, HIGHEST)
    blk = lax.broadcasted_iota(jnp.int32, (nbp, BS), 0)
    cnt = jnp.zeros((nbp, BS), F32)
    for j in range(nb):
        cj = bs_t[j:j + 1, :]
        beats = jnp.where(cj > bs_t, 1.0, jnp.where((cj == bs_t) & (blk > j), 1.0, 0.0))
        cnt = cnt + jnp.where(qi > j, beats, 0.0)
    sel_t = jnp.where((blk < qi) & (cnt < A_TOPK), 1.0, 0.0)
    sel = jnp.concatenate([sel_t, jnp.zeros((LANES - nbp, BS), F32)], axis=0).T.astype(BF16)

    qb = q.astype(BF16)
    row = lax.broadcasted_iota(jnp.int32, (BS, BS), 0)
    col = lax.broadcasted_iota(jnp.int32, (BS, BS), 1)
    own = jnp.where(row >= col, 1.0, 0.0)
    for n in range(1, nb + 1):
        @pl.when(qi == n - 1)
        def _(n=n):
            w = n * BS
            s = _dot_nt(qb, kb_ref[0:w, :])
            if n > 1:
                allowed = jnp.concatenate([_dot(sel, ex_ref[:, 0:w - BS]), own], axis=1)
            else:
                allowed = own
            s = jnp.where(allowed > 0.5, s, NEG_INF)
            m = jnp.max(s, axis=1, keepdims=True)
            p = jnp.exp(s - m)
            l = jnp.sum(p, axis=1, keepdims=True)
            y_ref[...] = (_dot(p.astype(BF16), vb_ref[0:w, :]) / l).astype(y_ref.dtype)


def moba_mixer(u, batch, seq):
    nq = seq // A_BLOCK
    nbp = -(-nq // 8) * 8
    cq, ck, cv = COL_AQ // LANES, COL_AK // LANES, COL_AV // LANES
    expand = (jnp.arange(LANES, dtype=jnp.int32)[:, None]
              == jnp.arange(seq, dtype=jnp.int32)[None, :] // A_BLOCK).astype(BF16)
    return pl.pallas_call(
        _moba_kernel,
        grid=(batch, A_HEADS, nq),
        in_specs=[pl.BlockSpec((A_BLOCK, A_HEAD_DIM), lambda b_, h, i: (b_ * nq + i, cq + h)),
                  pl.BlockSpec((seq, A_HEAD_DIM), lambda b_, h, i: (b_, ck + h)),
                  pl.BlockSpec((seq, A_HEAD_DIM), lambda b_, h, i: (b_, cv + h)),
                  pl.BlockSpec((LANES, seq), lambda b_, h, i: (0, 0))],
        out_specs=pl.BlockSpec((A_BLOCK, A_HEAD_DIM), lambda b_, h, i: (b_ * nq + i, h)),
        out_shape=jax.ShapeDtypeStruct((batch * seq, A_WIDTH), BF16),
        scratch_shapes=[pltpu.VMEM((seq, A_HEAD_DIM), BF16),
                        pltpu.VMEM((seq, A_HEAD_DIM), BF16),
                        pltpu.VMEM((nbp, A_HEAD_DIM), F32)],
        compiler_params=_params("arbitrary", "arbitrary", "arbitrary"),
        name="moba",
    )(u, u, u, expand)


def _out_proj_body(ym_ref, ys_ref, ya_ref, x_ref, w_ref, g_ref):
    x = x_ref[...]
    x = x + _dot(ym_ref[...], w_ref[0:M_WIDTH, :])
    x = x + _dot(ys_ref[...], w_ref[M_WIDTH:M_WIDTH + S_WIDTH, :])
    x = x + _dot(ya_ref[...], w_ref[M_WIDTH + S_WIDTH:, :])
    return x, _rms(x, g_ref[...])


def _out_proj_kernel(ym_ref, ys_ref, ya_ref, x_ref, w_ref, g_ref, xo_ref, hn_ref):
    x, hn = _out_proj_body(ym_ref, ys_ref, ya_ref, x_ref, w_ref, g_ref)
    xo_ref[...] = x
    hn_ref[...] = hn.astype(hn_ref.dtype)


def _out_proj_route_kernel(ym_ref, ys_ref, ya_ref, x_ref, w_ref, g_ref, wr_ref, xo_ref, hn_ref, route_ref):
    x, hn = _out_proj_body(ym_ref, ys_ref, ya_ref, x_ref, w_ref, g_ref)
    xo_ref[...] = x
    hn_ref[...] = hn
    hn_hi = hn.astype(BF16)
    hn_lo = (hn - hn_hi.astype(F32)).astype(BF16)
    logits = (_dot(hn_hi, wr_ref[0]) + _dot(hn_lo, wr_ref[0])
              + _dot(hn_hi, wr_ref[1]))
    lane = lax.broadcasted_iota(jnp.int32, logits.shape, 1)
    logits = jnp.where(lane < N_EXPERTS, logits, NEG_INF)
    m1 = jnp.max(logits, axis=1, keepdims=True)
    i1 = jnp.min(jnp.where(logits == m1, lane, LANES), axis=1, keepdims=True)
    rest = jnp.where(lane == i1, NEG_INF, logits)
    m2 = jnp.max(rest, axis=1, keepdims=True)
    i2 = jnp.min(jnp.where(rest == m2, lane, LANES), axis=1, keepdims=True)
    e2 = jnp.exp(m2 - m1)
    w1 = 1.0 / (1.0 + e2)
    w2 = e2 / (1.0 + e2)
    route_ref[...] = jnp.where(lane == 0, i1.astype(F32),
                               jnp.where(lane == 1, i2.astype(F32),
                                         jnp.where(lane == 2, w1, jnp.where(lane == 3, w2, 0.0))))


def out_proj(ym, ys, ya, x, w, g, w_router=None):
    t, d = x.shape
    tm = ROW_TILE
    row = lambda i: (i, 0)
    fixed = lambda i: (0, 0)
    in_specs = [pl.BlockSpec((tm, M_WIDTH), row), pl.BlockSpec((tm, S_WIDTH), row),
                pl.BlockSpec((tm, A_WIDTH), row), pl.BlockSpec((tm, d), row),
                pl.BlockSpec((d, d), fixed), pl.BlockSpec((1, d), fixed)]
    args = [ym, ys, ya, x, w, g.reshape(1, d)]
    if w_router is None:
        kern = _out_proj_kernel
        out_specs = [pl.BlockSpec((tm, d), row), pl.BlockSpec((tm, d), row)]
        out_shape = [jax.ShapeDtypeStruct((t, d), F32), jax.ShapeDtypeStruct((t, d), BF16)]
    else:
        kern = _out_proj_route_kernel
        in_specs.append(pl.BlockSpec((2, d, LANES), lambda i: (0, 0, 0)))
        args.append(w_router)
        out_specs = [pl.BlockSpec((tm, d), row), pl.BlockSpec((tm, d), row),
                     pl.BlockSpec((tm, LANES), row)]
        out_shape = [jax.ShapeDtypeStruct((t, d), F32), jax.ShapeDtypeStruct((t, d), F32),
                     jax.ShapeDtypeStruct((t, LANES), F32)]
    return pl.pallas_call(
        kern, grid=(t // tm,), in_specs=in_specs, out_specs=out_specs, out_shape=out_shape,
        compiler_params=_params("arbitrary"), name="out_proj",
    )(*args)


def _ffn_kernel(hn_ref, wg_ref, wu_ref, wd_ref, x_ref, g_ref, xo_ref, hp_ref, acc_ref):
    f = pl.program_id(1)

    @pl.when(f == 0)
    def _():
        acc_ref[...] = jnp.zeros_like(acc_ref)

    hn = hn_ref[...]
    h = _silu(_dot(hn, wg_ref[...])) * _dot(hn, wu_ref[...])
    acc_ref[...] += _dot(h.astype(BF16), wd_ref[...])

    @pl.when(f == pl.num_programs(1) - 1)
    def _():
        x = x_ref[...] + acc_ref[...]
        xo_ref[...] = x
        hp_ref[...] = _rms(x, g_ref[...]).astype(hp_ref.dtype)


def dense_ffn(hn, wg, wu, wd, x, g_next):
    t, d = x.shape
    ff = wg.shape[1]
    tm, tf = ROW_TILE, FF_TILE
    return pl.pallas_call(
        _ffn_kernel,
        grid=(t // tm, ff // tf),
        in_specs=[pl.BlockSpec((tm, d), lambda i, f: (i, 0)),
                  pl.BlockSpec((d, tf), lambda i, f: (0, f)),
                  pl.BlockSpec((d, tf), lambda i, f: (0, f)),
                  pl.BlockSpec((tf, d), lambda i, f: (f, 0)),
                  pl.BlockSpec((tm, d), lambda i, f: (i, 0)),
                  pl.BlockSpec((1, d), lambda i, f: (0, 0))],
        out_specs=[pl.BlockSpec((tm, d), lambda i, f: (i, 0)),
                   pl.BlockSpec((tm, d), lambda i, f: (i, 0))],
        out_shape=[jax.ShapeDtypeStruct((t, d), F32), jax.ShapeDtypeStruct((t, d), BF16)],
        scratch_shapes=[pltpu.VMEM((tm, d), F32)],
        compiler_params=_params("arbitrary", "arbitrary"),
        name="dense_ffn",
    )(hn, wg, wu, wd, x, g_next.reshape(1, d))


def _gather_rows_kernel(src_ref, h_hbm, o_ref, buf_ref, sem):
    i = pl.program_id(0)
    tg = o_ref.shape[0]
    base = i * tg

    def copy(r):
        return pltpu.make_async_copy(h_hbm.at[pl.ds(src_ref[base + r], 1)],
                                     buf_ref.at[pl.ds(r, 1)], sem)

    def start(r, carry):
        copy(r).start()
        return carry

    def wait(r, carry):
        copy(r).wait()
        return carry

    lax.fori_loop(0, tg, start, 0, unroll=DMA_UNROLL)
    lax.fori_loop(0, tg, wait, 0, unroll=DMA_UNROLL)
    o_ref[...] = buf_ref[...].astype(o_ref.dtype)


def gather_rows(h, src):
    t, d = h.shape
    r = src.shape[0]
    tg = GATHER_TILE
    return pl.pallas_call(
        _gather_rows_kernel,
        grid_spec=pltpu.PrefetchScalarGridSpec(
            num_scalar_prefetch=1,
            grid=(r // tg,),
            in_specs=[pl.BlockSpec(memory_space=pl.ANY)],
            out_specs=pl.BlockSpec((tg, d), lambda i, s: (i, 0)),
            scratch_shapes=[pltpu.VMEM((tg, d), F32), pltpu.SemaphoreType.DMA(())]),
        out_shape=jax.ShapeDtypeStruct((r, d), BF16),
        compiler_params=_params("arbitrary"),
        name="moe_dispatch",
    )(src, h)


def _expert_kernel(te_ref, nu_ref, xs_ref, wg_ref, wu_ref, wd_ref, y_ref, acc_ref):
    i = pl.program_id(0)
    f = pl.program_id(1)
    used = i < nu_ref[0]

    @pl.when(f == 0)
    def _():
        acc_ref[...] = jnp.zeros_like(acc_ref)

    @pl.when(used)
    def _():
        xs = xs_ref[...]
        h = _silu(_dot(xs, wg_ref[0])) * _dot(xs, wu_ref[0])
        acc_ref[...] += _dot(h.astype(BF16), wd_ref[0])

    @pl.when(f == pl.num_programs(1) - 1)
    def _():
        y_ref[...] = acc_ref[...]


def expert_ffn(xs, wg, wu, wd, tile_expert, n_used):
    r, d = xs.shape
    ff = wg.shape[2]
    tm, tf = MOE_ROW_TILE, MOE_FF_TILE
    nf = ff // tf

    def fblk(i, f, nu):
        return jnp.where(i < nu[0], f, nf - 1)

    return pl.pallas_call(
        _expert_kernel,
        grid_spec=pltpu.PrefetchScalarGridSpec(
            num_scalar_prefetch=2,
            grid=(r // tm, nf),
            in_specs=[pl.BlockSpec((tm, d), lambda i, f, te, nu: (i, 0)),
                      pl.BlockSpec((1, d, tf), lambda i, f, te, nu: (te[i], 0, fblk(i, f, nu))),
                      pl.BlockSpec((1, d, tf), lambda i, f, te, nu: (te[i], 0, fblk(i, f, nu))),
                      pl.BlockSpec((1, tf, d), lambda i, f, te, nu: (te[i], fblk(i, f, nu), 0))],
            out_specs=pl.BlockSpec((tm, d), lambda i, f, te, nu: (i, 0)),
            scratch_shapes=[pltpu.VMEM((tm, d), F32)]),
        out_shape=jax.ShapeDtypeStruct((r, d), F32),
        compiler_params=_params("arbitrary", "arbitrary"),
        name="moe_experts",
    )(tile_expert, n_used, xs, wg, wu, wd)


def _combine_kernel(pos_ref, y_hbm, x_ref, route_ref, g_ref, xo_ref, hp_ref, buf_ref, sem):
    i = pl.program_id(0)
    tc = x_ref.shape[0]
    t_total = pl.num_programs(0) * tc
    base = i * tc

    def copy(r, slot):
        return pltpu.make_async_copy(y_hbm.at[pl.ds(pos_ref[slot * t_total + base + r], 1)],
                                     buf_ref.at[slot, pl.ds(r, 1)], sem)

    def start(r, carry):
        copy(r, 0).start()
        copy(r, 1).start()
        return carry

    def wait(r, carry):
        copy(r, 0).wait()
        copy(r, 1).wait()
        return carry

    lax.fori_loop(0, tc, start, 0, unroll=DMA_UNROLL)
    lax.fori_loop(0, tc, wait, 0, unroll=DMA_UNROLL)
    route = route_ref[...]
    x = x_ref[...] + route[:, 2:3] * buf_ref[0] + route[:, 3:4] * buf_ref[1]
    xo_ref[...] = x
    hp_ref[...] = _rms(x, g_ref[...]).astype(hp_ref.dtype)


def moe_combine(y, pos, x, route, g_next):
    t, d = x.shape
    tc = GATHER_TILE
    return pl.pallas_call(
        _combine_kernel,
        grid_spec=pltpu.PrefetchScalarGridSpec(
            num_scalar_prefetch=1,
            grid=(t // tc,),
            in_specs=[pl.BlockSpec(memory_space=pl.ANY),
                      pl.BlockSpec((tc, d), lambda i, p_: (i, 0)),
                      pl.BlockSpec((tc, LANES), lambda i, p_: (i, 0)),
                      pl.BlockSpec((1, d), lambda i, p_: (0, 0))],
            out_specs=[pl.BlockSpec((tc, d), lambda i, p_: (i, 0)),
                       pl.BlockSpec((tc, d), lambda i, p_: (i, 0))],
            scratch_shapes=[pltpu.VMEM((2, tc, d), F32), pltpu.SemaphoreType.DMA(())]),
        out_shape=[jax.ShapeDtypeStruct((t, d), F32), jax.ShapeDtypeStruct((t, d), BF16)],
        compiler_params=_params("arbitrary"),
        name="moe_combine",
    )(pos, y, x, route, g_next.reshape(1, d))


def moe_ffn(hn, route, x, wg, wu, wd, g_next):
    t, d = x.shape
    tm = MOE_ROW_TILE
    n_tiles = (2 * t) // tm + N_EXPERTS
    e_flat = jnp.concatenate([route[:, 0], route[:, 1]]).astype(jnp.int32)
    onehot = (e_flat[:, None] == jnp.arange(N_EXPERTS, dtype=jnp.int32)[None, :]).astype(jnp.int32)
    counts = jnp.sum(onehot, axis=0)
    rank = jnp.sum((jnp.cumsum(onehot, axis=0) - onehot) * onehot, axis=1)
    padded = ((counts + tm - 1) // tm) * tm
    ends = jnp.cumsum(padded)
    offs = ends - padded
    pos = (jnp.sum(onehot * offs[None, :], axis=1) + rank).astype(jnp.int32)
    token = jnp.concatenate([jnp.arange(t, dtype=jnp.int32)] * 2)
    src = jnp.zeros((n_tiles * tm,), jnp.int32).at[pos].set(token)
    n_used = (ends[-1] // tm).astype(jnp.int32).reshape(1)
    tile_start = jnp.minimum(jnp.arange(n_tiles, dtype=jnp.int32), n_used[0] - 1) * tm
    tile_expert = jnp.sum((ends[None, :] <= tile_start[:, None]).astype(jnp.int32), axis=1)
    tile_expert = jnp.minimum(tile_expert, N_EXPERTS - 1).astype(jnp.int32)

    xs = gather_rows(hn, src)
    y = expert_ffn(xs, wg, wu, wd, tile_expert, n_used)
    return moe_combine(y, pos, x, route, g_next)


def _ple_kernel(hp_ref, p_ref, wg_ref, wp_ref, x_ref, g_ref, xo_ref, xn_ref):
    gate = jax.nn.sigmoid(_dot(hp_ref[...], wg_ref[...]))
    x = x_ref[...] + _dot(p_ref[...].astype(BF16), wp_ref[...]) * gate
    xo_ref[...] = x
    xn_ref[...] = _rms(x, g_ref[...]).astype(xn_ref.dtype)


def _ple_final_kernel(hp_ref, p_ref, wg_ref, wp_ref, x_ref, g_ref, o_ref):
    gate = jax.nn.sigmoid(_dot(hp_ref[...], wg_ref[...]))
    x = x_ref[...] + _dot(p_ref[...].astype(BF16), wp_ref[...]) * gate
    o_ref[...] = _rms(x, g_ref[...])


def ple(hp, p, wg, wp, x, g_next, final):
    t, d = x.shape
    dp = p.shape[1]
    tm = ROW_TILE
    row = lambda i: (i, 0)
    fixed = lambda i: (0, 0)
    in_specs = [pl.BlockSpec((tm, d), row), pl.BlockSpec((tm, dp), row),
                pl.BlockSpec((d, d), fixed), pl.BlockSpec((dp, d), fixed),
                pl.BlockSpec((tm, d), row), pl.BlockSpec((1, d), fixed)]
    if final:
        kern = _ple_final_kernel
        out_specs = pl.BlockSpec((tm, d), row)
        out_shape = jax.ShapeDtypeStruct((t, d), F32)
    else:
        kern = _ple_kernel
        out_specs = [pl.BlockSpec((tm, d), row), pl.BlockSpec((tm, d), row)]
        out_shape = [jax.ShapeDtypeStruct((t, d), F32), jax.ShapeDtypeStruct((t, d), BF16)]
    return pl.pallas_call(
        kern, grid=(t // tm,), in_specs=in_specs, out_specs=out_specs, out_shape=out_shape,
        compiler_params=_params("arbitrary"), name="ple",
    )(hp, p, wg, wp, x, g_next.reshape(1, d))


def _pack_w_in(w):
    sizes = (M_WIDTH, M_WIDTH, M_WIDTH, M_WIDTH, M_HEADS, M_HEADS, S_WIDTH, S_XBC, S_HEADS,
             A_WIDTH, A_WIDTH, A_WIDTH)
    offs = [0]
    for s in sizes:
        offs.append(offs[-1] + s)
    seg = [w[:, offs[i]:offs[i + 1]] for i in range(len(sizes))]
    mq, mk, mv, mo, mi, mf, sz, sxbc, sdt, aq, ak, av = seg
    pad = jnp.zeros((w.shape[0], LANES - 2 * M_HEADS - S_HEADS), w.dtype)
    return jnp.concatenate([mq, mk, mv, mo, sz, sxbc, aq, ak, av, mi, mf, sdt, pad], axis=1).astype(BF16)


def _gate_lanes(vec, lane0):
    return jnp.zeros((1, LANES), F32).at[0, lane0:lane0 + vec.shape[0]].set(vec.astype(F32))


def kernel(x, p, ln_mix, w_in, w_out, m_conv_w, m_conv_b, m_gate_b, m_norm_g, s_conv_w, s_conv_b, s_dt_bias, s_a_log, s_d, s_norm_g, ln_ffn, ffn_w_gate, ffn_w_up, ffn_w_down, moe_router, moe_w_gate, moe_w_up, moe_w_down, ln_ple, ple_proj, ple_gate, ln_final):
    batch, seq, d = x.shape
    depth = w_in.shape[0]
    t = batch * seq
    xf = x.reshape(t, d)
    qk_scale = jnp.concatenate([jnp.ones((M_WIDTH,), F32),
                                jnp.full((M_WIDTH,), M_HEAD_DIM ** -0.5, F32)])
    xbc_scale = jnp.ones((S_XBC,), F32)
    xn = rmsnorm_bf16(xf, ln_mix[0])
    out = None
    for i in range(depth):
        u = in_proj(xn, _pack_w_in(w_in[i]))
        qk = conv_silu(u, COL_MQ, 2 * M_WIDTH, m_conv_w[i], m_conv_b[i], qk_scale, batch, seq)
        xbc = conv_silu(u, COL_SXBC, S_XBC, s_conv_w[i], s_conv_b[i], xbc_scale, batch, seq)
        y_m = mlstm_mixer(u, qk, _gate_lanes(m_gate_b[i], GATE_LANE_MI), m_norm_g[i], batch, seq)
        y_s = ssd_mixer(u, xbc, _gate_lanes(s_dt_bias[i], GATE_LANE_SDT),
                        _gate_lanes(s_a_log[i], GATE_LANE_SDT),
                        jnp.repeat(s_d[i].astype(F32), S_HEAD_DIM), s_norm_g[i], batch, seq)
        y_a = moba_mixer(u, batch, seq)
        w_o = w_out[i].astype(BF16)
        j = i // 2
        if i % 2 == 0:
            xf, hn = out_proj(y_m, y_s, y_a, xf, w_o, ln_ffn[i])
            xf, hp = dense_ffn(hn, ffn_w_gate[j].astype(BF16), ffn_w_up[j].astype(BF16),
                               ffn_w_down[j].astype(BF16), xf, ln_ple[i])
        else:
            w_r = jnp.zeros((d, LANES), F32).at[:, :N_EXPERTS].set(moe_router[j].astype(F32))
            w_r_hi = w_r.astype(BF16)
            w_r = jnp.stack([w_r_hi, (w_r - w_r_hi.astype(F32)).astype(BF16)])
            xf, hn, route = out_proj(y_m, y_s, y_a, xf, w_o, ln_ffn[i], w_r)
            xf, hp = moe_ffn(hn, route, xf, moe_w_gate[j].astype(BF16), moe_w_up[j].astype(BF16),
                             moe_w_down[j].astype(BF16), ln_ple[i])
        final = i == depth - 1
        g_next = ln_final if final else ln_mix[i + 1]
        res = ple(hp, p[i].reshape(t, -1), ple_gate[i].astype(BF16), ple_proj[i].astype(BF16),
                  xf, g_next, final)
        if final:
            out = res
        else:
            xf, xn = res
    return out.reshape(batch, seq, d)
```

```python
import functools

import jax
import jax.numpy as jnp
from jax import lax
from jax.experimental import pallas as pl
from jax.experimental.pallas import tpu as pltpu

F32 = jnp.float32
BF16 = jnp.bfloat16
HIGHEST = lax.Precision.HIGHEST
NEG_INF = float("-inf")

EPS = 1e-6
LANES = 128
VMEM_LIMIT = 56 * 1024 * 1024

D_MODEL = 2048
M_HEADS = 4
M_HEAD_DIM = 128
M_WIDTH = M_HEADS * M_HEAD_DIM
S_HEADS = 16
S_HEAD_DIM = 64
S_WIDTH = S_HEADS * S_HEAD_DIM
S_GROUPS = 2
S_STATE = 128
S_XBC = S_WIDTH + 2 * S_GROUPS * S_STATE
S_GROUP_WIDTH = S_WIDTH // S_GROUPS
A_HEADS = 4
A_HEAD_DIM = 128
A_WIDTH = A_HEADS * A_HEAD_DIM
A_BLOCK = 256
A_TOPK = 3
N_EXPERTS = 8
CONV_K = 4

COL_MQ = 0
COL_MK = COL_MQ + M_WIDTH
COL_MV = COL_MK + M_WIDTH
COL_MO = COL_MV + M_WIDTH
COL_SZ = COL_MO + M_WIDTH
COL_SXBC = COL_SZ + S_WIDTH
COL_AQ = COL_SXBC + S_XBC
COL_AK = COL_AQ + A_WIDTH
COL_AV = COL_AK + A_WIDTH
D_PROJ_PACKED = COL_AV + A_WIDTH
GATE_LANE_MI = 0
GATE_LANE_MF = M_HEADS
GATE_LANE_SDT = 2 * M_HEADS

ROW_TILE = 512
PROJ_ROW_TILE = 1024
PROJ_COL_TILE = 1536
FFN_ROW_TILE = 1024
FF_TILE = 512
MOE_FF_TILE = 1024
M_CHUNK = 256
S_CHUNK = 128
MOE_ROW_TILE = 512
GATHER_TILE = 256
DMA_UNROLL = 8


def _params(*sem):
    return pltpu.CompilerParams(dimension_semantics=sem, vmem_limit_bytes=VMEM_LIMIT)


def _rms(x, g):
    return x * lax.rsqrt(jnp.mean(x * x, axis=-1, keepdims=True) + EPS) * g


def _dot(a, b, precision=None):
    return jnp.dot(a, b, preferred_element_type=F32, precision=precision)


def _dot_nt(a, b, precision=None):
    return lax.dot_general(a, b, (((1,), (1,)), ((), ())), preferred_element_type=F32,
                           precision=precision)


def _softplus(x):
    return jnp.maximum(x, 0.0) + jnp.log1p(jnp.exp(-jnp.abs(x)))


def _silu(x):
    return x * jax.nn.sigmoid(x)


def _rmsnorm_kernel(x_ref, g_ref, o_ref):
    o_ref[...] = _rms(x_ref[...], g_ref[...]).astype(o_ref.dtype)


def rmsnorm_bf16(x, g):
    t, d = x.shape
    return pl.pallas_call(
        _rmsnorm_kernel,
        grid=(t // ROW_TILE,),
        in_specs=[pl.BlockSpec((ROW_TILE, d), lambda i: (i, 0)),
                  pl.BlockSpec((1, d), lambda i: (0, 0))],
        out_specs=pl.BlockSpec((ROW_TILE, d), lambda i: (i, 0)),
        out_shape=jax.ShapeDtypeStruct((t, d), BF16),
        compiler_params=_params("arbitrary"),
        name="rmsnorm",
    )(x, g.reshape(1, d))


def _proj_kernel(x_ref, w_ref, o_ref):
    o_ref[...] = _dot(x_ref[...], w_ref[...]).astype(o_ref.dtype)


def in_proj(xn, w, out_dtype, col_tile):
    t, d = xn.shape
    n = w.shape[1]
    tm = PROJ_ROW_TILE
    return pl.pallas_call(
        _proj_kernel,
        grid=(n // col_tile, t // tm),
        in_specs=[pl.BlockSpec((tm, d), lambda j, i: (i, 0)),
                  pl.BlockSpec((d, col_tile), lambda j, i: (0, j))],
        out_specs=pl.BlockSpec((tm, col_tile), lambda j, i: (i, j)),
        out_shape=jax.ShapeDtypeStruct((t, n), out_dtype),
        compiler_params=_params("arbitrary", "arbitrary"),
        name="in_proj",
    )(xn, w)


def _conv_silu_kernel(u_ref, w_ref, b_ref, s_ref, o_ref):
    x = u_ref[...].astype(F32)
    row = lax.broadcasted_iota(jnp.int32, x.shape, 0)
    y = x * w_ref[CONV_K - 1:CONV_K, :] + b_ref[...]
    for shift in range(1, CONV_K):
        xs = jnp.where(row >= shift, pltpu.roll(x, shift, axis=0), 0.0)
        y = y + xs * w_ref[CONV_K - 1 - shift:CONV_K - shift, :]
    o_ref[...] = (_silu(y) * s_ref[...]).astype(o_ref.dtype)


def conv_silu(u, col0, width, w, b, post_scale, batch, seq):
    ct = 512
    cb0 = col0 // ct
    return pl.pallas_call(
        _conv_silu_kernel,
        grid=(batch, width // ct),
        in_specs=[pl.BlockSpec((seq, ct), lambda b_, c: (b_, cb0 + c)),
                  pl.BlockSpec((CONV_K, ct), lambda b_, c: (0, c)),
                  pl.BlockSpec((1, ct), lambda b_, c: (0, c)),
                  pl.BlockSpec((1, ct), lambda b_, c: (0, c))],
        out_specs=pl.BlockSpec((seq, ct), lambda b_, c: (b_, c)),
        out_shape=jax.ShapeDtypeStruct((batch * seq, width), BF16),
        compiler_params=_params("arbitrary", "arbitrary"),
        name="conv_silu",
    )(u, w, b.reshape(1, width), post_scale.reshape(1, width))


def _mlstm_kernel(q_ref, k_ref, v_ref, o_ref, g_ref, gb_ref, ng_ref, y_ref, ct_ref, n_ref, m_ref):
    c = pl.program_id(1)
    L = q_ref.shape[0]
    H, Dh = M_HEADS, M_HEAD_DIM

    @pl.when(c == 0)
    def _():
        ct_ref[...] = jnp.zeros_like(ct_ref)
        n_ref[...] = jnp.zeros_like(n_ref)
        m_ref[...] = jnp.zeros_like(m_ref)

    g = g_ref[...] + gb_ref[...]
    lf = jnp.minimum(g, 0.0) - jnp.log1p(jnp.exp(-jnp.abs(g)))
    row = lax.broadcasted_iota(jnp.int32, (L, L), 0)
    col = lax.broadcasted_iota(jnp.int32, (L, L), 1)
    causal = row >= col
    tril = jnp.where(causal, 1.0, 0.0).astype(F32)
    bcum = _dot(tril, lf, HIGHEST)
    g_t = g.T
    bcum_t = bcum.T
    for h in range(H):
        li_col = g[:, GATE_LANE_MI + h:GATE_LANE_MI + h + 1]
        li_row = g_t[GATE_LANE_MI + h:GATE_LANE_MI + h + 1, :]
        b_col = bcum[:, GATE_LANE_MF + h:GATE_LANE_MF + h + 1]
        b_row = bcum_t[GATE_LANE_MF + h:GATE_LANE_MF + h + 1, :]
        m_prev = m_ref[h:h + 1, 0:1]
        sl = slice(h * Dh, (h + 1) * Dh)
        qh = q_ref[:, sl]
        kh = k_ref[:, sl]
        vh = v_ref[:, sl]

        logw = jnp.where(causal, b_col - b_row + li_row, NEG_INF)
        m_inter = b_col + m_prev
        m_j = jnp.maximum(m_inter, jnp.max(logw, axis=1, keepdims=True))
        w = jnp.exp(logw - m_j)
        s_inter = jnp.exp(m_inter - m_j)
        sqk = _dot_nt(qh, kh) * w
        num = s_inter * _dot(qh, ct_ref[h].astype(BF16)) + _dot(sqk.astype(BF16), vh)
        n_row = n_ref[h][0:1, :]
        den = (s_inter * jnp.sum(qh.astype(F32) * n_row, axis=1, keepdims=True)
               + jnp.sum(sqk, axis=1, keepdims=True))
        hh = num / jnp.maximum(jnp.abs(den), jnp.exp(-m_j))

        total = b_col[L - 1:L, :]
        lw_end_row = total - b_row + li_row
        lw_end_col = total - b_col + li_col
        m_new = jnp.maximum(total + m_prev, jnp.max(lw_end_row, axis=1, keepdims=True))
        a_row = jnp.exp(lw_end_row - m_new)
        a_col = jnp.exp(lw_end_col - m_new)
        decay = jnp.exp(total + m_prev - m_new)
        ka_t = (kh.astype(F32) * a_col).T.astype(BF16)
        ct_ref[h] = decay * ct_ref[h] + _dot(ka_t, vh)
        a8 = jnp.broadcast_to(a_row, (8, L)).astype(BF16)
        n_ref[h] = decay * n_ref[h] + _dot(a8, kh)
        m_ref[h:h + 1, :] = jnp.broadcast_to(m_new, (1, LANES))

        hn = hh * lax.rsqrt(jnp.mean(hh * hh, axis=-1, keepdims=True) + EPS)
        o_gate = jax.nn.sigmoid(o_ref[:, sl].astype(F32))
        y_ref[:, sl] = (hn * ng_ref[:, sl] * o_gate).astype(y_ref.dtype)


def mlstm_mixer(u, gates, qk, gate_b, norm_g, batch, seq):
    L = M_CHUNK
    nc = seq // L
    w = M_WIDTH
    rows = lambda b_, c: b_ * nc + c
    return pl.pallas_call(
        _mlstm_kernel,
        grid=(batch, nc),
        in_specs=[pl.BlockSpec((L, w), lambda b_, c: (rows(b_, c), 0)),
                  pl.BlockSpec((L, w), lambda b_, c: (rows(b_, c), 1)),
                  pl.BlockSpec((L, w), lambda b_, c: (rows(b_, c), COL_MV // w)),
                  pl.BlockSpec((L, w), lambda b_, c: (rows(b_, c), COL_MO // w)),
                  pl.BlockSpec((L, LANES), lambda b_, c: (rows(b_, c), 0)),
                  pl.BlockSpec((1, LANES), lambda b_, c: (0, 0)),
                  pl.BlockSpec((1, w), lambda b_, c: (0, 0))],
        out_specs=pl.BlockSpec((L, w), lambda b_, c: (rows(b_, c), 0)),
        out_shape=jax.ShapeDtypeStruct((batch * seq, w), BF16),
        scratch_shapes=[pltpu.VMEM((M_HEADS, M_HEAD_DIM, M_HEAD_DIM), F32),
                        pltpu.VMEM((M_HEADS, 8, M_HEAD_DIM), F32),
                        pltpu.VMEM((8, LANES), F32)],
        compiler_params=_params("arbitrary", "arbitrary"),
        name="mlstm",
    )(qk, qk, u, u, gates, gate_b, norm_g.reshape(1, w))


def _ssd_kernel(xbc_ref, z_ref, g_ref, dtb_ref, alog_ref, dskip_ref, ng_ref, y_ref, s_ref):
    c = pl.program_id(1)
    L = xbc_ref.shape[0]
    GW, N, P = S_GROUP_WIDTH, S_STATE, S_HEAD_DIM
    E = S_HEADS // S_GROUPS

    @pl.when(c == 0)
    def _():
        s_ref[...] = jnp.zeros_like(s_ref)

    dt = _softplus(g_ref[...] + dtb_ref[...])
    a = dt * (-jnp.exp(alog_ref[...]))
    row = lax.broadcasted_iota(jnp.int32, (L, L), 0)
    col = lax.broadcasted_iota(jnp.int32, (L, L), 1)
    causal = row >= col
    tril = jnp.where(causal, 1.0, 0.0).astype(F32)
    acum = _dot(tril, a, HIGHEST)
    acum_t = acum.T
    lane_p = lax.broadcasted_iota(jnp.int32, (L, LANES), 1)
    heads_per_slab = LANES // P
    for gi in range(S_GROUPS):
        gs = slice(gi * GW, (gi + 1) * GW)
        x = xbc_ref[:, gs].astype(F32)
        bm = xbc_ref[:, S_WIDTH + gi * N:S_WIDTH + (gi + 1) * N]
        cm = xbc_ref[:, S_WIDTH + S_GROUPS * N + gi * N:S_WIDTH + S_GROUPS * N + (gi + 1) * N]
        cb = _dot_nt(cm, bm)
        pieces, xdt_slabs, ac_slabs = [], [], []
        for pi in range(GW // LANES):
            x_slab = x[:, pi * LANES:(pi + 1) * LANES]
            dt_slab = ac_slab = None
            per_head = []
            for half in range(heads_per_slab):
                ln = GATE_LANE_SDT + gi * E + pi * heads_per_slab + half
                ac_col = jnp.broadcast_to(acum[:, ln:ln + 1], (L, LANES))
                dt_col = jnp.broadcast_to(dt[:, ln:ln + 1], (L, LANES))
                in_half = lane_p >= half * P
                dt_slab = dt_col if dt_slab is None else jnp.where(in_half, dt_col, dt_slab)
                ac_slab = ac_col if ac_slab is None else jnp.where(in_half, ac_col, ac_slab)
                seg = jnp.broadcast_to(acum[:, ln:ln + 1], (L, L)) - acum_t[ln:ln + 1, :]
                per_head.append(jnp.exp(jnp.where(causal, seg, NEG_INF)))
            xdt_slab = x_slab * dt_slab
            acc = None
            for half in range(heads_per_slab):
                in_half = (lane_p >= half * P) & (lane_p < (half + 1) * P)
                d = _dot((cb * per_head[half]).astype(BF16),
                         jnp.where(in_half, xdt_slab, 0.0).astype(BF16))
                acc = d if acc is None else acc + d
            pieces.append(acc)
            xdt_slabs.append(xdt_slab)
            ac_slabs.append(ac_slab)
        y = jnp.concatenate(pieces, axis=1)
        xdt = jnp.concatenate(xdt_slabs, axis=1)
        ac_x = jnp.concatenate(ac_slabs, axis=1)
        tot_x = ac_x[L - 1:L, :]
        state = s_ref[gi]
        y = y + _dot(cm, state.astype(BF16)) * jnp.exp(ac_x) + x * dskip_ref[:, gs]
        xdd = (xdt * jnp.exp(tot_x - ac_x)).astype(BF16)
        bm_t = bm.astype(F32).T.astype(BF16)
        s_ref[gi] = state * jnp.exp(tot_x) + _dot(bm_t, xdd)
        y = y * _silu(z_ref[:, gs].astype(F32))
        y_ref[:, gs] = _rms(y, ng_ref[:, gs]).astype(y_ref.dtype)


def ssd_mixer(u, gates, xbc, dt_bias, a_log, d_skip, norm_g, batch, seq):
    L = S_CHUNK
    nc = seq // L
    rows = lambda b_, c: b_ * nc + c
    return pl.pallas_call(
        _ssd_kernel,
        grid=(batch, nc),
        in_specs=[pl.BlockSpec((L, S_XBC), lambda b_, c: (rows(b_, c), 0)),
                  pl.BlockSpec((L, S_WIDTH), lambda b_, c: (rows(b_, c), COL_SZ // S_WIDTH)),
                  pl.BlockSpec((L, LANES), lambda b_, c: (rows(b_, c), 0)),
                  pl.BlockSpec((1, LANES), lambda b_, c: (0, 0)),
                  pl.BlockSpec((1, LANES), lambda b_, c: (0, 0)),
                  pl.BlockSpec((1, S_WIDTH), lambda b_, c: (0, 0)),
                  pl.BlockSpec((1, S_WIDTH), lambda b_, c: (0, 0))],
        out_specs=pl.BlockSpec((L, S_WIDTH), lambda b_, c: (rows(b_, c), 0)),
        out_shape=jax.ShapeDtypeStruct((batch * seq, S_WIDTH), BF16),
        scratch_shapes=[pltpu.VMEM((S_GROUPS, S_STATE, S_GROUP_WIDTH), F32)],
        compiler_params=_params("arbitrary", "arbitrary"),
        name="ssd",
    )(xbc, u, gates, dt_bias, a_log, d_skip.reshape(1, S_WIDTH), norm_g.reshape(1, S_WIDTH))


def _moba_kernel(q_ref, k_ref, v_ref, ex_ref, y_ref, kmean_ref):
    qi = pl.program_id(2)
    BS = A_BLOCK
    S = k_ref.shape[0]
    nb = S // BS
    nbp = kmean_ref.shape[0]
    Dh = A_HEAD_DIM

    @pl.when(qi == 0)
    def _():
        k = k_ref[...].astype(F32)
        k_mean = jnp.sum(k.reshape(nb, BS, Dh), axis=1) * (1.0 / BS)
        if nbp > nb:
            k_mean = jnp.concatenate([k_mean, jnp.zeros((nbp - nb, Dh), F32)], axis=0)
        kmean_ref[...] = k_mean

    q = q_ref[...].astype(F32) * (Dh ** -0.5)
    bs_t = _dot_nt(kmean_ref[...], q, HIGHEST)
    blk = lax.broadcasted_iota(jnp.int32, (nbp, BS), 0)
    cnt = jnp.zeros((nbp, BS), F32)
    for j in range(nb):
        cj = bs_t[j:j + 1, :]
        beats = jnp.where(cj > bs_t, 1.0, jnp.where((cj == bs_t) & (blk > j), 1.0, 0.0))
        cnt = cnt + jnp.where(qi > j, beats, 0.0)
    sel_t = jnp.where((blk < qi) & (cnt < A_TOPK), 1.0, 0.0)
    sel = jnp.concatenate([sel_t, jnp.zeros((LANES - nbp, BS), F32)], axis=0).T.astype(BF16)

    qb = q.astype(BF16)
    row = lax.broadcasted_iota(jnp.int32, (BS, BS), 0)
    col = lax.broadcasted_iota(jnp.int32, (BS, BS), 1)
    own = jnp.where(row >= col, 1.0, 0.0)
    for n in range(1, nb + 1):
        @pl.when(qi == n - 1)
        def _(n=n):
            w = n * BS
            s = _dot_nt(qb, k_ref[0:w, :])
            if n > 1:
                allowed = jnp.concatenate([_dot(sel, ex_ref[:, 0:w - BS]), own], axis=1)
            else:
                allowed = own
            s = jnp.where(allowed > 0.5, s, NEG_INF)
            m = jnp.max(s, axis=1, keepdims=True)
            p = jnp.exp(s - m)
            l = jnp.sum(p, axis=1, keepdims=True)
            y_ref[...] = (_dot(p.astype(BF16), v_ref[0:w, :]) / l).astype(y_ref.dtype)


def moba_mixer(u, batch, seq):
    nq = seq // A_BLOCK
    nbp = -(-nq // 8) * 8
    cq, ck, cv = COL_AQ // LANES, COL_AK // LANES, COL_AV // LANES
    expand = (jnp.arange(LANES, dtype=jnp.int32)[:, None]
              == jnp.arange(seq, dtype=jnp.int32)[None, :] // A_BLOCK).astype(BF16)
    return pl.pallas_call(
        _moba_kernel,
        grid=(batch, A_HEADS, nq),
        in_specs=[pl.BlockSpec((A_BLOCK, A_HEAD_DIM), lambda b_, h, i: (b_ * nq + i, cq + h)),
                  pl.BlockSpec((seq, A_HEAD_DIM), lambda b_, h, i: (b_, ck + h)),
                  pl.BlockSpec((seq, A_HEAD_DIM), lambda b_, h, i: (b_, cv + h)),
                  pl.BlockSpec((LANES, seq), lambda b_, h, i: (0, 0))],
        out_specs=pl.BlockSpec((A_BLOCK, A_HEAD_DIM), lambda b_, h, i: (b_ * nq + i, h)),
        out_shape=jax.ShapeDtypeStruct((batch * seq, A_WIDTH), BF16),
        scratch_shapes=[pltpu.VMEM((nbp, A_HEAD_DIM), F32)],
        compiler_params=_params("arbitrary", "arbitrary", "arbitrary"),
        name="moba",
    )(u, u, u, expand)


def _out_proj_body(ym_ref, ys_ref, ya_ref, x_ref, w_ref, g_ref):
    x = x_ref[...]
    x = x + _dot(ym_ref[...], w_ref[0:M_WIDTH, :])
    x = x + _dot(ys_ref[...], w_ref[M_WIDTH:M_WIDTH + S_WIDTH, :])
    x = x + _dot(ya_ref[...], w_ref[M_WIDTH + S_WIDTH:, :])
    return x, _rms(x, g_ref[...])


def _out_proj_kernel(ym_ref, ys_ref, ya_ref, x_ref, w_ref, g_ref, xo_ref, hn_ref):
    x, hn = _out_proj_body(ym_ref, ys_ref, ya_ref, x_ref, w_ref, g_ref)
    xo_ref[...] = x
    hn_ref[...] = hn.astype(hn_ref.dtype)


def _out_proj_route_kernel(ym_ref, ys_ref, ya_ref, x_ref, w_ref, g_ref, wr_ref, xo_ref, hn_ref, route_ref):
    x, hn = _out_proj_body(ym_ref, ys_ref, ya_ref, x_ref, w_ref, g_ref)
    xo_ref[...] = x
    hn_ref[...] = hn
    hn_hi = hn.astype(BF16)
    hn_lo = (hn - hn_hi.astype(F32)).astype(BF16)
    logits = (_dot(hn_hi, wr_ref[0]) + _dot(hn_lo, wr_ref[0])
              + _dot(hn_hi, wr_ref[1]))
    lane = lax.broadcasted_iota(jnp.int32, logits.shape, 1)
    logits = jnp.where(lane < N_EXPERTS, logits, NEG_INF)
    m1 = jnp.max(logits, axis=1, keepdims=True)
    i1 = jnp.min(jnp.where(logits == m1, lane, LANES), axis=1, keepdims=True)
    rest = jnp.where(lane == i1, NEG_INF, logits)
    m2 = jnp.max(rest, axis=1, keepdims=True)
    i2 = jnp.min(jnp.where(rest == m2, lane, LANES), axis=1, keepdims=True)
    e2 = jnp.exp(m2 - m1)
    w1 = 1.0 / (1.0 + e2)
    w2 = e2 / (1.0 + e2)
    route_ref[...] = jnp.where(lane == 0, i1.astype(F32),
                               jnp.where(lane == 1, i2.astype(F32),
                                         jnp.where(lane == 2, w1, jnp.where(lane == 3, w2, 0.0))))


def out_proj(ym, ys, ya, x, w, g, w_router=None):
    t, d = x.shape
    tm = ROW_TILE
    row = lambda i: (i, 0)
    fixed = lambda i: (0, 0)
    in_specs = [pl.BlockSpec((tm, M_WIDTH), row), pl.BlockSpec((tm, S_WIDTH), row),
                pl.BlockSpec((tm, A_WIDTH), row), pl.BlockSpec((tm, d), row),
                pl.BlockSpec((d, d), fixed), pl.BlockSpec((1, d), fixed)]
    args = [ym, ys, ya, x, w, g.reshape(1, d)]
    if w_router is None:
        kern = _out_proj_kernel
        out_specs = [pl.BlockSpec((tm, d), row), pl.BlockSpec((tm, d), row)]
        out_shape = [jax.ShapeDtypeStruct((t, d), F32), jax.ShapeDtypeStruct((t, d), BF16)]
    else:
        kern = _out_proj_route_kernel
        in_specs.append(pl.BlockSpec((2, d, LANES), lambda i: (0, 0, 0)))
        args.append(w_router)
        out_specs = [pl.BlockSpec((tm, d), row), pl.BlockSpec((tm, d), row),
                     pl.BlockSpec((tm, LANES), row)]
        out_shape = [jax.ShapeDtypeStruct((t, d), F32), jax.ShapeDtypeStruct((t, d), F32),
                     jax.ShapeDtypeStruct((t, LANES), F32)]
    return pl.pallas_call(
        kern, grid=(t // tm,), in_specs=in_specs, out_specs=out_specs, out_shape=out_shape,
        compiler_params=_params("arbitrary"), name="out_proj",
    )(*args)


def _ffn_kernel(hn_ref, wg_ref, wu_ref, wd_ref, o_ref):
    @pl.when(pl.program_id(1) == 0)
    def _():
        o_ref[...] = jnp.zeros_like(o_ref)

    hn = hn_ref[...]
    h = _silu(_dot(hn, wg_ref[...])) * _dot(hn, wu_ref[...])
    o_ref[...] += _dot(h.astype(BF16), wd_ref[...])


def dense_ffn(hn, wg, wu, wd):
    t, d = hn.shape
    ff = wg.shape[1]
    tm, tf = FFN_ROW_TILE, FF_TILE
    return pl.pallas_call(
        _ffn_kernel,
        grid=(t // tm, ff // tf),
        in_specs=[pl.BlockSpec((tm, d), lambda i, f: (i, 0)),
                  pl.BlockSpec((d, tf), lambda i, f: (0, f)),
                  pl.BlockSpec((d, tf), lambda i, f: (0, f)),
                  pl.BlockSpec((tf, d), lambda i, f: (f, 0))],
        out_specs=pl.BlockSpec((tm, d), lambda i, f: (i, 0)),
        out_shape=jax.ShapeDtypeStruct((t, d), F32),
        compiler_params=_params("arbitrary", "arbitrary"),
        name="dense_ffn",
    )(hn, wg, wu, wd)


def _gather_rows_kernel(src_ref, h_hbm, o_ref, buf_ref, sem):
    i = pl.program_id(0)
    tg = o_ref.shape[0]
    base = i * tg

    def copy(r):
        return pltpu.make_async_copy(h_hbm.at[pl.ds(src_ref[base + r], 1)],
                                     buf_ref.at[pl.ds(r, 1)], sem)

    def start(r, carry):
        copy(r).start()
        return carry

    def wait(r, carry):
        copy(r).wait()
        return carry

    lax.fori_loop(0, tg, start, 0, unroll=DMA_UNROLL)
    lax.fori_loop(0, tg, wait, 0, unroll=DMA_UNROLL)
    o_ref[...] = buf_ref[...].astype(o_ref.dtype)


def gather_rows(h, src):
    t, d = h.shape
    r = src.shape[0]
    tg = GATHER_TILE
    return pl.pallas_call(
        _gather_rows_kernel,
        grid_spec=pltpu.PrefetchScalarGridSpec(
            num_scalar_prefetch=1,
            grid=(r // tg,),
            in_specs=[pl.BlockSpec(memory_space=pl.ANY)],
            out_specs=pl.BlockSpec((tg, d), lambda i, s: (i, 0)),
            scratch_shapes=[pltpu.VMEM((tg, d), F32), pltpu.SemaphoreType.DMA(())]),
        out_shape=jax.ShapeDtypeStruct((r, d), BF16),
        compiler_params=_params("arbitrary"),
        name="moe_dispatch",
    )(src, h)


def _expert_kernel(te_ref, nu_ref, xs_ref, wg_ref, wu_ref, wd_ref, y_ref, acc_ref):
    i = pl.program_id(0)
    f = pl.program_id(1)
    used = i < nu_ref[0]

    @pl.when(f == 0)
    def _():
        acc_ref[...] = jnp.zeros_like(acc_ref)

    @pl.when(used)
    def _():
        xs = xs_ref[...]
        h = _silu(_dot(xs, wg_ref[0])) * _dot(xs, wu_ref[0])
        acc_ref[...] += _dot(h.astype(BF16), wd_ref[0])

    @pl.when(f == pl.num_programs(1) - 1)
    def _():
        y_ref[...] = acc_ref[...]


def expert_ffn(xs, wg, wu, wd, tile_expert, n_used):
    r, d = xs.shape
    ff = wg.shape[2]
    tm, tf = MOE_ROW_TILE, MOE_FF_TILE
    nf = ff // tf

    def fblk(i, f, nu):
        return jnp.where(i < nu[0], f, nf - 1)

    return pl.pallas_call(
        _expert_kernel,
        grid_spec=pltpu.PrefetchScalarGridSpec(
            num_scalar_prefetch=2,
            grid=(r // tm, nf),
            in_specs=[pl.BlockSpec((tm, d), lambda i, f, te, nu: (i, 0)),
                      pl.BlockSpec((1, d, tf), lambda i, f, te, nu: (te[i], 0, fblk(i, f, nu))),
                      pl.BlockSpec((1, d, tf), lambda i, f, te, nu: (te[i], 0, fblk(i, f, nu))),
                      pl.BlockSpec((1, tf, d), lambda i, f, te, nu: (te[i], fblk(i, f, nu), 0))],
            out_specs=pl.BlockSpec((tm, d), lambda i, f, te, nu: (i, 0)),
            scratch_shapes=[pltpu.VMEM((tm, d), F32)]),
        out_shape=jax.ShapeDtypeStruct((r, d), F32),
        compiler_params=_params("arbitrary", "arbitrary"),
        name="moe_experts",
    )(tile_expert, n_used, xs, wg, wu, wd)


def _combine_kernel(pos_ref, y_hbm, x_ref, route_ref, g_ref, xo_ref, hp_ref, buf_ref, sem):
    i = pl.program_id(0)
    tc = x_ref.shape[0]
    t_total = pl.num_programs(0) * tc
    base = i * tc

    def copy(r, slot):
        return pltpu.make_async_copy(y_hbm.at[pl.ds(pos_ref[slot * t_total + base + r], 1)],
                                     buf_ref.at[slot, pl.ds(r, 1)], sem)

    def start(r, carry):
        copy(r, 0).start()
        copy(r, 1).start()
        return carry

    def wait(r, carry):
        copy(r, 0).wait()
        copy(r, 1).wait()
        return carry

    lax.fori_loop(0, tc, start, 0, unroll=DMA_UNROLL)
    lax.fori_loop(0, tc, wait, 0, unroll=DMA_UNROLL)
    route = route_ref[...]
    x = x_ref[...] + route[:, 2:3] * buf_ref[0] + route[:, 3:4] * buf_ref[1]
    xo_ref[...] = x
    hp_ref[...] = _rms(x, g_ref[...]).astype(hp_ref.dtype)


def moe_combine(y, pos, x, route, g_next):
    t, d = x.shape
    tc = GATHER_TILE
    return pl.pallas_call(
        _combine_kernel,
        grid_spec=pltpu.PrefetchScalarGridSpec(
            num_scalar_prefetch=1,
            grid=(t // tc,),
            in_specs=[pl.BlockSpec(memory_space=pl.ANY),
                      pl.BlockSpec((tc, d), lambda i, p_: (i, 0)),
                      pl.BlockSpec((tc, LANES), lambda i, p_: (i, 0)),
                      pl.BlockSpec((1, d), lambda i, p_: (0, 0))],
            out_specs=[pl.BlockSpec((tc, d), lambda i, p_: (i, 0)),
                       pl.BlockSpec((tc, d), lambda i, p_: (i, 0))],
            scratch_shapes=[pltpu.VMEM((2, tc, d), F32), pltpu.SemaphoreType.DMA(())]),
        out_shape=[jax.ShapeDtypeStruct((t, d), F32), jax.ShapeDtypeStruct((t, d), BF16)],
        compiler_params=_params("arbitrary"),
        name="moe_combine",
    )(pos, y, x, route, g_next.reshape(1, d))


def moe_ffn(hn, route, x, wg, wu, wd, g_next):
    t, d = x.shape
    tm = MOE_ROW_TILE
    n_tiles = (2 * t) // tm + N_EXPERTS
    e_flat = jnp.concatenate([route[:, 0], route[:, 1]]).astype(jnp.int32)
    onehot = (e_flat[:, None] == jnp.arange(N_EXPERTS, dtype=jnp.int32)[None, :]).astype(jnp.int32)
    counts = jnp.sum(onehot, axis=0)
    rank = jnp.sum((jnp.cumsum(onehot, axis=0) - onehot) * onehot, axis=1)
    padded = ((counts + tm - 1) // tm) * tm
    ends = jnp.cumsum(padded)
    offs = ends - padded
    pos = (jnp.sum(onehot * offs[None, :], axis=1) + rank).astype(jnp.int32)
    token = jnp.concatenate([jnp.arange(t, dtype=jnp.int32)] * 2)
    src = jnp.zeros((n_tiles * tm,), jnp.int32).at[pos].set(token)
    n_used = (ends[-1] // tm).astype(jnp.int32).reshape(1)
    tile_start = jnp.minimum(jnp.arange(n_tiles, dtype=jnp.int32), n_used[0] - 1) * tm
    tile_expert = jnp.sum((ends[None, :] <= tile_start[:, None]).astype(jnp.int32), axis=1)
    tile_expert = jnp.minimum(tile_expert, N_EXPERTS - 1).astype(jnp.int32)

    xs = gather_rows(hn, src)
    y = expert_ffn(xs, wg, wu, wd, tile_expert, n_used)
    return moe_combine(y, pos, x, route, g_next)


def _ple_kernel(pending_ffn, final, *refs):
    if pending_ffn:
        h_ref, gp_ref, p_ref, wg_ref, wp_ref, x_ref, g_ref, *outs = refs
        x = x_ref[...] + h_ref[...]
        hp = _rms(x, gp_ref[...]).astype(BF16)
    else:
        h_ref, p_ref, wg_ref, wp_ref, x_ref, g_ref, *outs = refs
        x = x_ref[...]
        hp = h_ref[...]
    gate = jax.nn.sigmoid(_dot(hp, wg_ref[...]))
    x = x + _dot(p_ref[...].astype(BF16), wp_ref[...]) * gate
    if final:
        outs[0][...] = _rms(x, g_ref[...])
    else:
        outs[0][...] = x
        outs[1][...] = _rms(x, g_ref[...]).astype(outs[1].dtype)


def ple(h, g_ple, p, wg, wp, x, g_next, final):
    t, d = x.shape
    dp = p.shape[1]
    tm = ROW_TILE
    row = lambda i: (i, 0)
    fixed = lambda i: (0, 0)
    pending_ffn = g_ple is not None
    in_specs = [pl.BlockSpec((tm, d), row)]
    args = [h]
    if pending_ffn:
        in_specs.append(pl.BlockSpec((1, d), fixed))
        args.append(g_ple.reshape(1, d))
    in_specs += [pl.BlockSpec((tm, dp), row), pl.BlockSpec((d, d), fixed), pl.BlockSpec((dp, d), fixed),
                 pl.BlockSpec((tm, d), row), pl.BlockSpec((1, d), fixed)]
    args += [p, wg, wp, x, g_next.reshape(1, d)]
    if final:
        out_specs = [pl.BlockSpec((tm, d), row)]
        out_shape = [jax.ShapeDtypeStruct((t, d), F32)]
    else:
        out_specs = [pl.BlockSpec((tm, d), row), pl.BlockSpec((tm, d), row)]
        out_shape = [jax.ShapeDtypeStruct((t, d), F32), jax.ShapeDtypeStruct((t, d), BF16)]
    return pl.pallas_call(
        functools.partial(_ple_kernel, pending_ffn, final),
        grid=(t // tm,), in_specs=in_specs, out_specs=out_specs, out_shape=out_shape,
        compiler_params=_params("arbitrary"), name="ple",
    )(*args)


def _pack_w_in(w):
    sizes = (M_WIDTH, M_WIDTH, M_WIDTH, M_WIDTH, M_HEADS, M_HEADS, S_WIDTH, S_XBC, S_HEADS,
             A_WIDTH, A_WIDTH, A_WIDTH)
    offs = [0]
    for s in sizes:
        offs.append(offs[-1] + s)
    seg = [w[:, offs[i]:offs[i + 1]] for i in range(len(sizes))]
    mq, mk, mv, mo, mi, mf, sz, sxbc, sdt, aq, ak, av = seg
    pad = jnp.zeros((w.shape[0], LANES - 2 * M_HEADS - S_HEADS), w.dtype)
    main = jnp.concatenate([mq, mk, mv, mo, sz, sxbc, aq, ak, av], axis=1).astype(BF16)
    gates = jnp.concatenate([mi, mf, sdt, pad], axis=1).astype(BF16)
    return main, gates


def _gate_lanes(vec, lane0):
    return jnp.zeros((1, LANES), F32).at[0, lane0:lane0 + vec.shape[0]].set(vec.astype(F32))


def kernel(x, p, ln_mix, w_in, w_out, m_conv_w, m_conv_b, m_gate_b, m_norm_g, s_conv_w, s_conv_b, s_dt_bias, s_a_log, s_d, s_norm_g, ln_ffn, ffn_w_gate, ffn_w_up, ffn_w_down, moe_router, moe_w_gate, moe_w_up, moe_w_down, ln_ple, ple_proj, ple_gate, ln_final):
    batch, seq, d = x.shape
    depth = w_in.shape[0]
    t = batch * seq
    xf = x.reshape(t, d)
    qk_scale = jnp.concatenate([jnp.ones((M_WIDTH,), F32),
                                jnp.full((M_WIDTH,), M_HEAD_DIM ** -0.5, F32)])
    xbc_scale = jnp.ones((S_XBC,), F32)
    xn = rmsnorm_bf16(xf, ln_mix[0])
    out = None
    for i in range(depth):
        w_main, w_gates = _pack_w_in(w_in[i])
        u = in_proj(xn, w_main, BF16, PROJ_COL_TILE)
        gates = in_proj(xn, w_gates, F32, LANES)
        qk = conv_silu(u, COL_MQ, 2 * M_WIDTH, m_conv_w[i], m_conv_b[i], qk_scale, batch, seq)
        xbc = conv_silu(u, COL_SXBC, S_XBC, s_conv_w[i], s_conv_b[i], xbc_scale, batch, seq)
        y_m = mlstm_mixer(u, gates, qk, _gate_lanes(m_gate_b[i], GATE_LANE_MI), m_norm_g[i], batch, seq)
        y_s = ssd_mixer(u, gates, xbc, _gate_lanes(s_dt_bias[i], GATE_LANE_SDT),
                        _gate_lanes(s_a_log[i], GATE_LANE_SDT),
                        jnp.repeat(s_d[i].astype(F32), S_HEAD_DIM), s_norm_g[i], batch, seq)
        y_a = moba_mixer(u, batch, seq)
        w_o = w_out[i].astype(BF16)
        j = i // 2
        if i % 2 == 0:
            xf, hn = out_proj(y_m, y_s, y_a, xf, w_o, ln_ffn[i])
            hp = dense_ffn(hn, ffn_w_gate[j].astype(BF16), ffn_w_up[j].astype(BF16),
                           ffn_w_down[j].astype(BF16))
            g_ple = ln_ple[i]
        else:
            w_r = jnp.zeros((d, LANES), F32).at[:, :N_EXPERTS].set(moe_router[j].astype(F32))
            w_r_hi = w_r.astype(BF16)
            w_r = jnp.stack([w_r_hi, (w_r - w_r_hi.astype(F32)).astype(BF16)])
            xf, hn, route = out_proj(y_m, y_s, y_a, xf, w_o, ln_ffn[i], w_r)
            xf, hp = moe_ffn(hn, route, xf, moe_w_gate[j].astype(BF16), moe_w_up[j].astype(BF16),
                             moe_w_down[j].astype(BF16), ln_ple[i])
            g_ple = None
        final = i == depth - 1
        g_next = ln_final if final else ln_mix[i + 1]
        res = ple(hp, g_ple, p[i].reshape(t, -1), ple_gate[i].astype(BF16), ple_proj[i].astype(BF16),
                  xf, g_next, final)
        if final:
            out = res[0]
        else:
            xf, xn = res
    return out.reshape(batch, seq, d)
```

```python
import functools

import jax
import jax.numpy as jnp
from jax import lax
from jax.experimental import pallas as pl
from jax.experimental.pallas import tpu as pltpu

F32 = jnp.float32
BF16 = jnp.bfloat16
HIGHEST = lax.Precision.HIGHEST
NEG_INF = float("-inf")

EPS = 1e-6
LANES = 128
VMEM_LIMIT = 56 * 1024 * 1024

D_MODEL = 2048
M_HEADS = 4
M_HEAD_DIM = 128
M_WIDTH = M_HEADS * M_HEAD_DIM
S_HEADS = 16
S_HEAD_DIM = 64
S_WIDTH = S_HEADS * S_HEAD_DIM
S_GROUPS = 2
S_STATE = 128
S_XBC = S_WIDTH + 2 * S_GROUPS * S_STATE
S_GROUP_WIDTH = S_WIDTH // S_GROUPS
A_HEADS = 4
A_HEAD_DIM = 128
A_WIDTH = A_HEADS * A_HEAD_DIM
A_BLOCK = 256
A_TOPK = 3
N_EXPERTS = 8
CONV_K = 4

COL_MQ = 0
COL_MK = COL_MQ + M_WIDTH
COL_MV = COL_MK + M_WIDTH
COL_MO = COL_MV + M_WIDTH
COL_SZ = COL_MO + M_WIDTH
COL_SXBC = COL_SZ + S_WIDTH
COL_AQ = COL_SXBC + S_XBC
COL_AK = COL_AQ + A_WIDTH
COL_AV = COL_AK + A_WIDTH
D_PROJ_PACKED = COL_AV + A_WIDTH
GATE_LANE_MI = 0
GATE_LANE_MF = M_HEADS
GATE_LANE_SDT = 2 * M_HEADS

ROW_TILE = 512
PROJ_ROW_TILE = 1024
PROJ_COL_TILE = 1536
FFN_ROW_TILE = 1024
FF_TILE = 512
MOE_FF_TILE = 1024
M_CHUNK = 256
S_CHUNK = 128
MOE_ROW_TILE = 512
GATHER_TILE = 256
DMA_UNROLL = 8


def _params(*sem):
    return pltpu.CompilerParams(dimension_semantics=sem, vmem_limit_bytes=VMEM_LIMIT)


def _rms(x, g):
    return x * lax.rsqrt(jnp.mean(x * x, axis=-1, keepdims=True) + EPS) * g


def _dot(a, b, precision=None):
    return jnp.dot(a, b, preferred_element_type=F32, precision=precision)


def _dot_nt(a, b, precision=None):
    return lax.dot_general(a, b, (((1,), (1,)), ((), ())), preferred_element_type=F32,
                           precision=precision)


def _softplus(x):
    return jnp.maximum(x, 0.0) + jnp.log1p(jnp.exp(-jnp.abs(x)))


def _silu(x):
    return x * jax.nn.sigmoid(x)


def _rmsnorm_kernel(x_ref, g_ref, o_ref):
    o_ref[...] = _rms(x_ref[...], g_ref[...]).astype(o_ref.dtype)


def rmsnorm_bf16(x, g):
    t, d = x.shape
    return pl.pallas_call(
        _rmsnorm_kernel,
        grid=(t // ROW_TILE,),
        in_specs=[pl.BlockSpec((ROW_TILE, d), lambda i: (i, 0)),
                  pl.BlockSpec((1, d), lambda i: (0, 0))],
        out_specs=pl.BlockSpec((ROW_TILE, d), lambda i: (i, 0)),
        out_shape=jax.ShapeDtypeStruct((t, d), BF16),
        compiler_params=_params("arbitrary"),
        name="rmsnorm",
    )(x, g.reshape(1, d))


def _proj_kernel(x_ref, w_ref, o_ref):
    o_ref[...] = _dot(x_ref[...], w_ref[...]).astype(o_ref.dtype)


def in_proj(xn, w, out_dtype, col_tile):
    t, d = xn.shape
    n = w.shape[1]
    tm = PROJ_ROW_TILE
    return pl.pallas_call(
        _proj_kernel,
        grid=(n // col_tile, t // tm),
        in_specs=[pl.BlockSpec((tm, d), lambda j, i: (i, 0)),
                  pl.BlockSpec((d, col_tile), lambda j, i: (0, j))],
        out_specs=pl.BlockSpec((tm, col_tile), lambda j, i: (i, j)),
        out_shape=jax.ShapeDtypeStruct((t, n), out_dtype),
        compiler_params=_params("arbitrary", "arbitrary"),
        name="in_proj",
    )(xn, w)


def _conv_silu_kernel(u_ref, w_ref, b_ref, s_ref, o_ref):
    x = u_ref[...].astype(F32)
    row = lax.broadcasted_iota(jnp.int32, x.shape, 0)
    y = x * w_ref[CONV_K - 1:CONV_K, :] + b_ref[...]
    for shift in range(1, CONV_K):
        xs = jnp.where(row >= shift, pltpu.roll(x, shift, axis=0), 0.0)
        y = y + xs * w_ref[CONV_K - 1 - shift:CONV_K - shift, :]
    o_ref[...] = (_silu(y) * s_ref[...]).astype(o_ref.dtype)


def conv_silu(u, col0, width, w, b, post_scale, batch, seq):
    ct = 512
    cb0 = col0 // ct
    return pl.pallas_call(
        _conv_silu_kernel,
        grid=(batch, width // ct),
        in_specs=[pl.BlockSpec((seq, ct), lambda b_, c: (b_, cb0 + c)),
                  pl.BlockSpec((CONV_K, ct), lambda b_, c: (0, c)),
                  pl.BlockSpec((1, ct), lambda b_, c: (0, c)),
                  pl.BlockSpec((1, ct), lambda b_, c: (0, c))],
        out_specs=pl.BlockSpec((seq, ct), lambda b_, c: (b_, c)),
        out_shape=jax.ShapeDtypeStruct((batch * seq, width), BF16),
        compiler_params=_params("arbitrary", "arbitrary"),
        name="conv_silu",
    )(u, w, b.reshape(1, width), post_scale.reshape(1, width))


def _mlstm_kernel(q_ref, k_ref, v_ref, o_ref, g_ref, gb_ref, ng_ref, y_ref, ct_ref, n_ref, m_ref):
    c = pl.program_id(1)
    L = q_ref.shape[0]
    H, Dh = M_HEADS, M_HEAD_DIM

    @pl.when(c == 0)
    def _():
        ct_ref[...] = jnp.zeros_like(ct_ref)
        n_ref[...] = jnp.zeros_like(n_ref)
        m_ref[...] = jnp.zeros_like(m_ref)

    g = g_ref[...] + gb_ref[...]
    lf = jnp.minimum(g, 0.0) - jnp.log1p(jnp.exp(-jnp.abs(g)))
    row = lax.broadcasted_iota(jnp.int32, (L, L), 0)
    col = lax.broadcasted_iota(jnp.int32, (L, L), 1)
    causal = row >= col
    tril = jnp.where(causal, 1.0, 0.0).astype(F32)
    bcum = _dot(tril, lf, HIGHEST)
    g_t = g.T
    bcum_t = bcum.T
    for h in range(H):
        li_col = g[:, GATE_LANE_MI + h:GATE_LANE_MI + h + 1]
        li_row = g_t[GATE_LANE_MI + h:GATE_LANE_MI + h + 1, :]
        b_col = bcum[:, GATE_LANE_MF + h:GATE_LANE_MF + h + 1]
        b_row = bcum_t[GATE_LANE_MF + h:GATE_LANE_MF + h + 1, :]
        m_prev = m_ref[h:h + 1, 0:1]
        sl = slice(h * Dh, (h + 1) * Dh)
        qh = q_ref[:, sl]
        kh = k_ref[:, sl]
        vh = v_ref[:, sl]

        logw = jnp.where(causal, b_col - b_row + li_row, NEG_INF)
        m_inter = b_col + m_prev
        m_j = jnp.maximum(m_inter, jnp.max(logw, axis=1, keepdims=True))
        w = jnp.exp(logw - m_j)
        s_inter = jnp.exp(m_inter - m_j)
        sqk = _dot_nt(qh, kh) * w
        num = s_inter * _dot(qh, ct_ref[h].astype(BF16)) + _dot(sqk.astype(BF16), vh)
        n_row = n_ref[h][0:1, :]
        den = (s_inter * jnp.sum(qh.astype(F32) * n_row, axis=1, keepdims=True)
               + jnp.sum(sqk, axis=1, keepdims=True))
        hh = num / jnp.maximum(jnp.abs(den), jnp.exp(-m_j))

        total = b_col[L - 1:L, :]
        lw_end_row = total - b_row + li_row
        lw_end_col = total - b_col + li_col
        m_new = jnp.maximum(total + m_prev, jnp.max(lw_end_row, axis=1, keepdims=True))
        a_row = jnp.exp(lw_end_row - m_new)
        a_col = jnp.exp(lw_end_col - m_new)
        decay = jnp.exp(total + m_prev - m_new)
        ka_t = (kh.astype(F32) * a_col).T.astype(BF16)
        ct_ref[h] = decay * ct_ref[h] + _dot(ka_t, vh)
        a8 = jnp.broadcast_to(a_row, (8, L)).astype(BF16)
        n_ref[h] = decay * n_ref[h] + _dot(a8, kh)
        m_ref[h:h + 1, :] = jnp.broadcast_to(m_new, (1, LANES))

        hn = hh * lax.rsqrt(jnp.mean(hh * hh, axis=-1, keepdims=True) + EPS)
        o_gate = jax.nn.sigmoid(o_ref[:, sl].astype(F32))
        y_ref[:, sl] = (hn * ng_ref[:, sl] * o_gate).astype(y_ref.dtype)


def mlstm_mixer(u, gates, qk, gate_b, norm_g, batch, seq):
    L = M_CHUNK
    nc = seq // L
    w = M_WIDTH
    rows = lambda b_, c: b_ * nc + c
    return pl.pallas_call(
        _mlstm_kernel,
        grid=(batch, nc),
        in_specs=[pl.BlockSpec((L, w), lambda b_, c: (rows(b_, c), 0)),
                  pl.BlockSpec((L, w), lambda b_, c: (rows(b_, c), 1)),
                  pl.BlockSpec((L, w), lambda b_, c: (rows(b_, c), COL_MV // w)),
                  pl.BlockSpec((L, w), lambda b_, c: (rows(b_, c), COL_MO // w)),
                  pl.BlockSpec((L, LANES), lambda b_, c: (rows(b_, c), 0)),
                  pl.BlockSpec((1, LANES), lambda b_, c: (0, 0)),
                  pl.BlockSpec((1, w), lambda b_, c: (0, 0))],
        out_specs=pl.BlockSpec((L, w), lambda b_, c: (rows(b_, c), 0)),
        out_shape=jax.ShapeDtypeStruct((batch * seq, w), BF16),
        scratch_shapes=[pltpu.VMEM((M_HEADS, M_HEAD_DIM, M_HEAD_DIM), F32),
                        pltpu.VMEM((M_HEADS, 8, M_HEAD_DIM), F32),
                        pltpu.VMEM((8, LANES), F32)],
        compiler_params=_params("arbitrary", "arbitrary"),
        name="mlstm",
    )(qk, qk, u, u, gates, gate_b, norm_g.reshape(1, w))


def _ssd_kernel(xbc_ref, z_ref, g_ref, dtb_ref, alog_ref, dskip_ref, ng_ref, y_ref, s_ref):
    c = pl.program_id(1)
    L = xbc_ref.shape[0]
    GW, N, P = S_GROUP_WIDTH, S_STATE, S_HEAD_DIM
    E = S_HEADS // S_GROUPS

    @pl.when(c == 0)
    def _():
        s_ref[...] = jnp.zeros_like(s_ref)

    dt = _softplus(g_ref[...] + dtb_ref[...])
    a = dt * (-jnp.exp(alog_ref[...]))
    row = lax.broadcasted_iota(jnp.int32, (L, L), 0)
    col = lax.broadcasted_iota(jnp.int32, (L, L), 1)
    causal = row >= col
    tril = jnp.where(causal, 1.0, 0.0).astype(F32)
    acum = _dot(tril, a, HIGHEST)
    acum_t = acum.T
    lane_p = lax.broadcasted_iota(jnp.int32, (L, LANES), 1)
    heads_per_slab = LANES // P
    for gi in range(S_GROUPS):
        gs = slice(gi * GW, (gi + 1) * GW)
        x = xbc_ref[:, gs].astype(F32)
        bm = xbc_ref[:, S_WIDTH + gi * N:S_WIDTH + (gi + 1) * N]
        cm = xbc_ref[:, S_WIDTH + S_GROUPS * N + gi * N:S_WIDTH + S_GROUPS * N + (gi + 1) * N]
        cb = _dot_nt(cm, bm)
        pieces, xdt_slabs, ac_slabs = [], [], []
        for pi in range(GW // LANES):
            x_slab = x[:, pi * LANES:(pi + 1) * LANES]
            dt_slab = ac_slab = None
            per_head = []
            for half in range(heads_per_slab):
                ln = GATE_LANE_SDT + gi * E + pi * heads_per_slab + half
                ac_col = jnp.broadcast_to(acum[:, ln:ln + 1], (L, LANES))
                dt_col = jnp.broadcast_to(dt[:, ln:ln + 1], (L, LANES))
                in_half = lane_p >= half * P
                dt_slab = dt_col if dt_slab is None else jnp.where(in_half, dt_col, dt_slab)
                ac_slab = ac_col if ac_slab is None else jnp.where(in_half, ac_col, ac_slab)
                seg = jnp.broadcast_to(acum[:, ln:ln + 1], (L, L)) - acum_t[ln:ln + 1, :]
                per_head.append(jnp.exp(jnp.where(causal, seg, NEG_INF)))
            xdt_slab = x_slab * dt_slab
            acc = None
            for half in range(heads_per_slab):
                in_half = (lane_p >= half * P) & (lane_p < (half + 1) * P)
                d = _dot((cb * per_head[half]).astype(BF16),
                         jnp.where(in_half, xdt_slab, 0.0).astype(BF16))
                acc = d if acc is None else acc + d
            pieces.append(acc)
            xdt_slabs.append(xdt_slab)
            ac_slabs.append(ac_slab)
        y = jnp.concatenate(pieces, axis=1)
        xdt = jnp.concatenate(xdt_slabs, axis=1)
        ac_x = jnp.concatenate(ac_slabs, axis=1)
        tot_x = ac_x[L - 1:L, :]
        state = s_ref[gi]
        y = y + _dot(cm, state.astype(BF16)) * jnp.exp(ac_x) + x * dskip_ref[:, gs]
        xdd = (xdt * jnp.exp(tot_x - ac_x)).astype(BF16)
        bm_t = bm.astype(F32).T.astype(BF16)
        s_ref[gi] = state * jnp.exp(tot_x) + _dot(bm_t, xdd)
        y = y * _silu(z_ref[:, gs].astype(F32))
        y_ref[:, gs] = _rms(y, ng_ref[:, gs]).astype(y_ref.dtype)


def ssd_mixer(u, gates, xbc, dt_bias, a_log, d_skip, norm_g, batch, seq):
    L = S_CHUNK
    nc = seq // L
    rows = lambda b_, c: b_ * nc + c
    return pl.pallas_call(
        _ssd_kernel,
        grid=(batch, nc),
        in_specs=[pl.BlockSpec((L, S_XBC), lambda b_, c: (rows(b_, c), 0)),
                  pl.BlockSpec((L, S_WIDTH), lambda b_, c: (rows(b_, c), COL_SZ // S_WIDTH)),
                  pl.BlockSpec((L, LANES), lambda b_, c: (rows(b_, c), 0)),
                  pl.BlockSpec((1, LANES), lambda b_, c: (0, 0)),
                  pl.BlockSpec((1, LANES), lambda b_, c: (0, 0)),
                  pl.BlockSpec((1, S_WIDTH), lambda b_, c: (0, 0)),
                  pl.BlockSpec((1, S_WIDTH), lambda b_, c: (0, 0))],
        out_specs=pl.BlockSpec((L, S_WIDTH), lambda b_, c: (rows(b_, c), 0)),
        out_shape=jax.ShapeDtypeStruct((batch * seq, S_WIDTH), BF16),
        scratch_shapes=[pltpu.VMEM((S_GROUPS, S_STATE, S_GROUP_WIDTH), F32)],
        compiler_params=_params("arbitrary", "arbitrary"),
        name="ssd",
    )(xbc, u, gates, dt_bias, a_log, d_skip.reshape(1, S_WIDTH), norm_g.reshape(1, S_WIDTH))


def _moba_kernel(q_ref, k_ref, v_ref, ex_ref, y_ref, kmean_ref):
    qi = pl.program_id(2)
    BS = A_BLOCK
    S = k_ref.shape[0]
    nb = S // BS
    nbp = kmean_ref.shape[0]
    Dh = A_HEAD_DIM

    @pl.when(qi == 0)
    def _():
        k = k_ref[...].astype(F32)
        k_mean = jnp.sum(k.reshape(nb, BS, Dh), axis=1) * (1.0 / BS)
        if nbp > nb:
            k_mean = jnp.concatenate([k_mean, jnp.zeros((nbp - nb, Dh), F32)], axis=0)
        kmean_ref[...] = k_mean

    q = q_ref[...].astype(F32) * (Dh ** -0.5)
    bs_t = _dot_nt(kmean_ref[...], q, HIGHEST)
    blk = lax.broadcasted_iota(jnp.int32, (nbp, BS), 0)
    cnt = jnp.zeros((nbp, BS), F32)
    for j in range(nb):
        cj = bs_t[j:j + 1, :]
        beats = jnp.where(cj > bs_t, 1.0, jnp.where((cj == bs_t) & (blk > j), 1.0, 0.0))
        cnt = cnt + jnp.where(qi > j, beats, 0.0)
    sel_t = jnp.where((blk < qi) & (cnt < A_TOPK), 1.0, 0.0)
    sel = jnp.concatenate([sel_t, jnp.zeros((LANES - nbp, BS), F32)], axis=0).T.astype(BF16)

    qb = q.astype(BF16)
    row = lax.broadcasted_iota(jnp.int32, (BS, BS), 0)
    col = lax.broadcasted_iota(jnp.int32, (BS, BS), 1)
    own = jnp.where(row >= col, 1.0, 0.0)
    for n in range(1, nb + 1):
        @pl.when(qi == n - 1)
        def _(n=n):
            w = n * BS
            s = _dot_nt(qb, k_ref[0:w, :])
            if n > 1:
                allowed = jnp.concatenate([_dot(sel, ex_ref[:, 0:w - BS]), own], axis=1)
            else:
                allowed = own
            s = jnp.where(allowed > 0.5, s, NEG_INF)
            m = jnp.max(s, axis=1, keepdims=True)
            p = jnp.exp(s - m)
            l = jnp.sum(p, axis=1, keepdims=True)
            y_ref[...] = (_dot(p.astype(BF16), v_ref[0:w, :]) / l).astype(y_ref.dtype)


def moba_mixer(u, batch, seq):
    nq = seq // A_BLOCK
    nbp = -(-nq // 8) * 8
    cq, ck, cv = COL_AQ // LANES, COL_AK // LANES, COL_AV // LANES
    expand = (jnp.arange(LANES, dtype=jnp.int32)[:, None]
              == jnp.arange(seq, dtype=jnp.int32)[None, :] // A_BLOCK).astype(BF16)
    return pl.pallas_call(
        _moba_kernel,
        grid=(batch, A_HEADS, nq),
        in_specs=[pl.BlockSpec((A_BLOCK, A_HEAD_DIM), lambda b_, h, i: (b_ * nq + i, cq + h)),
                  pl.BlockSpec((seq, A_HEAD_DIM), lambda b_, h, i: (b_, ck + h)),
                  pl.BlockSpec((seq, A_HEAD_DIM), lambda b_, h, i: (b_, cv + h)),
                  pl.BlockSpec((LANES, seq), lambda b_, h, i: (0, 0))],
        out_specs=pl.BlockSpec((A_BLOCK, A_HEAD_DIM), lambda b_, h, i: (b_ * nq + i, h)),
        out_shape=jax.ShapeDtypeStruct((batch * seq, A_WIDTH), BF16),
        scratch_shapes=[pltpu.VMEM((nbp, A_HEAD_DIM), F32)],
        compiler_params=_params("arbitrary", "arbitrary", "arbitrary"),
        name="moba",
    )(u, u, u, expand)


def _out_proj_body(ym_ref, ys_ref, ya_ref, x_ref, w_ref, g_ref):
    x = x_ref[...]
    x = x + _dot(ym_ref[...], w_ref[0:M_WIDTH, :])
    x = x + _dot(ys_ref[...], w_ref[M_WIDTH:M_WIDTH + S_WIDTH, :])
    x = x + _dot(ya_ref[...], w_ref[M_WIDTH + S_WIDTH:, :])
    return x, _rms(x, g_ref[...])


def _out_proj_kernel(ym_ref, ys_ref, ya_ref, x_ref, w_ref, g_ref, xo_ref, hn_ref):
    x, hn = _out_proj_body(ym_ref, ys_ref, ya_ref, x_ref, w_ref, g_ref)
    xo_ref[...] = x
    hn_ref[...] = hn.astype(hn_ref.dtype)


def _out_proj_route_kernel(ym_ref, ys_ref, ya_ref, x_ref, w_ref, g_ref, wr_ref, xo_ref, hn_ref, route_ref):
    x, hn = _out_proj_body(ym_ref, ys_ref, ya_ref, x_ref, w_ref, g_ref)
    xo_ref[...] = x
    hn_ref[...] = hn
    hn_hi = hn.astype(BF16)
    hn_lo = (hn - hn_hi.astype(F32)).astype(BF16)
    logits = (_dot(hn_hi, wr_ref[0]) + _dot(hn_lo, wr_ref[0])
              + _dot(hn_hi, wr_ref[1]))
    lane = lax.broadcasted_iota(jnp.int32, logits.shape, 1)
    logits = jnp.where(lane < N_EXPERTS, logits, NEG_INF)
    m1 = jnp.max(logits, axis=1, keepdims=True)
    i1 = jnp.min(jnp.where(logits == m1, lane, LANES), axis=1, keepdims=True)
    rest = jnp.where(lane == i1, NEG_INF, logits)
    m2 = jnp.max(rest, axis=1, keepdims=True)
    i2 = jnp.min(jnp.where(rest == m2, lane, LANES), axis=1, keepdims=True)
    e2 = jnp.exp(m2 - m1)
    w1 = 1.0 / (1.0 + e2)
    w2 = e2 / (1.0 + e2)
    route_ref[...] = jnp.where(lane == 0, i1.astype(F32),
                               jnp.where(lane == 1, i2.astype(F32),
                                         jnp.where(lane == 2, w1, jnp.where(lane == 3, w2, 0.0))))


def out_proj(ym, ys, ya, x, w, g, w_router=None):
    t, d = x.shape
    tm = ROW_TILE
    row = lambda i: (i, 0)
    fixed = lambda i: (0, 0)
    in_specs = [pl.BlockSpec((tm, M_WIDTH), row), pl.BlockSpec((tm, S_WIDTH), row),
                pl.BlockSpec((tm, A_WIDTH), row), pl.BlockSpec((tm, d), row),
                pl.BlockSpec((d, d), fixed), pl.BlockSpec((1, d), fixed)]
    args = [ym, ys, ya, x, w, g.reshape(1, d)]
    if w_router is None:
        kern = _out_proj_kernel
        out_specs = [pl.BlockSpec((tm, d), row), pl.BlockSpec((tm, d), row)]
        out_shape = [jax.ShapeDtypeStruct((t, d), F32), jax.ShapeDtypeStruct((t, d), BF16)]
    else:
        kern = _out_proj_route_kernel
        in_specs.append(pl.BlockSpec((2, d, LANES), lambda i: (0, 0, 0)))
        args.append(w_router)
        out_specs = [pl.BlockSpec((tm, d), row), pl.BlockSpec((tm, d), row),
                     pl.BlockSpec((tm, LANES), row)]
        out_shape = [jax.ShapeDtypeStruct((t, d), F32), jax.ShapeDtypeStruct((t, d), F32),
                     jax.ShapeDtypeStruct((t, LANES), F32)]
    return pl.pallas_call(
        kern, grid=(t // tm,), in_specs=in_specs, out_specs=out_specs, out_shape=out_shape,
        compiler_params=_params("arbitrary"), name="out_proj",
    )(*args)


def _ffn_kernel(hn_ref, wg_ref, wu_ref, wd_ref, o_ref):
    @pl.when(pl.program_id(1) == 0)
    def _():
        o_ref[...] = jnp.zeros_like(o_ref)

    hn = hn_ref[...]
    h = _silu(_dot(hn, wg_ref[...])) * _dot(hn, wu_ref[...])
    o_ref[...] += _dot(h.astype(BF16), wd_ref[...])


def dense_ffn(hn, wg, wu, wd):
    t, d = hn.shape
    ff = wg.shape[1]
    tm, tf = FFN_ROW_TILE, FF_TILE
    return pl.pallas_call(
        _ffn_kernel,
        grid=(t // tm, ff // tf),
        in_specs=[pl.BlockSpec((tm, d), lambda i, f: (i, 0)),
                  pl.BlockSpec((d, tf), lambda i, f: (0, f)),
                  pl.BlockSpec((d, tf), lambda i, f: (0, f)),
                  pl.BlockSpec((tf, d), lambda i, f: (f, 0))],
        out_specs=pl.BlockSpec((tm, d), lambda i, f: (i, 0)),
        out_shape=jax.ShapeDtypeStruct((t, d), F32),
        compiler_params=_params("arbitrary", "arbitrary"),
        name="dense_ffn",
    )(hn, wg, wu, wd)


def _expert_kernel(src_ref, te_ref, nu_ref, h_hbm, wg_ref, wu_ref, wd_ref, y_ref, gbuf_ref, xs_ref, sem):
    i = pl.program_id(0)
    f = pl.program_id(1)
    tm = y_ref.shape[0]
    n_used = nu_ref[0]
    used = i < n_used
    slot = lax.rem(i, 2)

    def row_copy(tile, r, slot_):
        return pltpu.make_async_copy(h_hbm.at[pl.ds(src_ref[tile * tm + r], 1)],
                                     gbuf_ref.at[slot_, pl.ds(r, 1)], sem.at[slot_])

    def start_tile(tile, slot_):
        def body(r, carry):
            row_copy(tile, r, slot_).start()
            return carry
        lax.fori_loop(0, tm, body, 0, unroll=DMA_UNROLL)

    def wait_tile(tile, slot_):
        def body(r, carry):
            row_copy(tile, r, slot_).wait()
            return carry
        lax.fori_loop(0, tm, body, 0, unroll=DMA_UNROLL)

    @pl.when(f == 0)
    def _():
        y_ref[...] = jnp.zeros_like(y_ref)

    @pl.when((f == 0) & used)
    def _():
        @pl.when(i == 0)
        def _():
            start_tile(0, 0)

        wait_tile(i, slot)
        xs_ref[...] = gbuf_ref[slot].astype(xs_ref.dtype)

        @pl.when(i + 1 < n_used)
        def _():
            start_tile(i + 1, 1 - slot)

    @pl.when(used)
    def _():
        xs = xs_ref[...]
        h = _silu(_dot(xs, wg_ref[0])) * _dot(xs, wu_ref[0])
        y_ref[...] += _dot(h.astype(BF16), wd_ref[0])


def expert_ffn(h, src, wg, wu, wd, tile_expert, n_used):
    t, d = h.shape
    r = src.shape[0]
    ff = wg.shape[2]
    tm, tf = MOE_ROW_TILE, MOE_FF_TILE
    nf = ff // tf

    def fblk(i, f, nu):
        return jnp.where(i < nu[0], f, nf - 1)

    return pl.pallas_call(
        _expert_kernel,
        grid_spec=pltpu.PrefetchScalarGridSpec(
            num_scalar_prefetch=3,
            grid=(r // tm, nf),
            in_specs=[pl.BlockSpec(memory_space=pl.ANY),
                      pl.BlockSpec((1, d, tf), lambda i, f, s, te, nu: (te[i], 0, fblk(i, f, nu))),
                      pl.BlockSpec((1, d, tf), lambda i, f, s, te, nu: (te[i], 0, fblk(i, f, nu))),
                      pl.BlockSpec((1, tf, d), lambda i, f, s, te, nu: (te[i], fblk(i, f, nu), 0))],
            out_specs=pl.BlockSpec((tm, d), lambda i, f, s, te, nu: (i, 0)),
            scratch_shapes=[pltpu.VMEM((2, tm, d), F32), pltpu.VMEM((tm, d), BF16),
                            pltpu.SemaphoreType.DMA((2,))]),
        out_shape=jax.ShapeDtypeStruct((r, d), F32),
        compiler_params=_params("arbitrary", "arbitrary"),
        name="moe_experts",
    )(src, tile_expert, n_used, h, wg, wu, wd)


def _combine_kernel(pos_ref, y_hbm, x_ref, route_ref, g_ref, xo_ref, hp_ref, buf_ref, sem):
    i = pl.program_id(0)
    n = pl.num_programs(0)
    tc = x_ref.shape[0]
    t_total = n * tc
    slot = lax.rem(i, 2)

    def copy(tile, r, choice, slot_):
        return pltpu.make_async_copy(y_hbm.at[pl.ds(pos_ref[choice * t_total + tile * tc + r], 1)],
                                     buf_ref.at[slot_, choice, pl.ds(r, 1)], sem.at[slot_])

    def start_tile(tile, slot_):
        def body(r, carry):
            copy(tile, r, 0, slot_).start()
            copy(tile, r, 1, slot_).start()
            return carry
        lax.fori_loop(0, tc, body, 0, unroll=DMA_UNROLL)

    def wait_tile(tile, slot_):
        def body(r, carry):
            copy(tile, r, 0, slot_).wait()
            copy(tile, r, 1, slot_).wait()
            return carry
        lax.fori_loop(0, tc, body, 0, unroll=DMA_UNROLL)

    @pl.when(i == 0)
    def _():
        start_tile(0, 0)

    @pl.when(i + 1 < n)
    def _():
        start_tile(i + 1, 1 - slot)

    wait_tile(i, slot)
    route = route_ref[...]
    x = x_ref[...] + route[:, 2:3] * buf_ref[slot, 0] + route[:, 3:4] * buf_ref[slot, 1]
    xo_ref[...] = x
    hp_ref[...] = _rms(x, g_ref[...]).astype(hp_ref.dtype)


def moe_combine(y, pos, x, route, g_next):
    t, d = x.shape
    tc = GATHER_TILE
    return pl.pallas_call(
        _combine_kernel,
        grid_spec=pltpu.PrefetchScalarGridSpec(
            num_scalar_prefetch=1,
            grid=(t // tc,),
            in_specs=[pl.BlockSpec(memory_space=pl.ANY),
                      pl.BlockSpec((tc, d), lambda i, p_: (i, 0)),
                      pl.BlockSpec((tc, LANES), lambda i, p_: (i, 0)),
                      pl.BlockSpec((1, d), lambda i, p_: (0, 0))],
            out_specs=[pl.BlockSpec((tc, d), lambda i, p_: (i, 0)),
                       pl.BlockSpec((tc, d), lambda i, p_: (i, 0))],
            scratch_shapes=[pltpu.VMEM((2, 2, tc, d), F32), pltpu.SemaphoreType.DMA((2,))]),
        out_shape=[jax.ShapeDtypeStruct((t, d), F32), jax.ShapeDtypeStruct((t, d), BF16)],
        compiler_params=_params("arbitrary"),
        name="moe_combine",
    )(pos, y, x, route, g_next.reshape(1, d))


def moe_ffn(hn, route, x, wg, wu, wd, g_next):
    t, d = x.shape
    tm = MOE_ROW_TILE
    n_tiles = (2 * t) // tm + N_EXPERTS
    e_flat = jnp.concatenate([route[:, 0], route[:, 1]]).astype(jnp.int32)
    onehot = (e_flat[:, None] == jnp.arange(N_EXPERTS, dtype=jnp.int32)[None, :]).astype(jnp.int32)
    counts = jnp.sum(onehot, axis=0)
    rank = jnp.sum((jnp.cumsum(onehot, axis=0) - onehot) * onehot, axis=1)
    padded = ((counts + tm - 1) // tm) * tm
    ends = jnp.cumsum(padded)
    offs = ends - padded
    pos = (jnp.sum(onehot * offs[None, :], axis=1) + rank).astype(jnp.int32)
    token = jnp.concatenate([jnp.arange(t, dtype=jnp.int32)] * 2)
    src = jnp.zeros((n_tiles * tm,), jnp.int32).at[pos].set(token)
    n_used = (ends[-1] // tm).astype(jnp.int32).reshape(1)
    tile_start = jnp.minimum(jnp.arange(n_tiles, dtype=jnp.int32), n_used[0] - 1) * tm
    tile_expert = jnp.sum((ends[None, :] <= tile_start[:, None]).astype(jnp.int32), axis=1)
    tile_expert = jnp.minimum(tile_expert, N_EXPERTS - 1).astype(jnp.int32)

    y = expert_ffn(hn, src, wg, wu, wd, tile_expert, n_used)
    return moe_combine(y, pos, x, route, g_next)


def _ple_kernel(pending_ffn, final, *refs):
    if pending_ffn:
        h_ref, gp_ref, p_ref, wg_ref, wp_ref, x_ref, g_ref, *outs = refs
        x = x_ref[...] + h_ref[...]
        hp = _rms(x, gp_ref[...]).astype(BF16)
    else:
        h_ref, p_ref, wg_ref, wp_ref, x_ref, g_ref, *outs = refs
        x = x_ref[...]
        hp = h_ref[...]
    gate = jax.nn.sigmoid(_dot(hp, wg_ref[...]))
    x = x + _dot(p_ref[...].astype(BF16), wp_ref[...]) * gate
    if final:
        outs[0][...] = _rms(x, g_ref[...])
    else:
        outs[0][...] = x
        outs[1][...] = _rms(x, g_ref[...]).astype(outs[1].dtype)


def ple(h, g_ple, p, wg, wp, x, g_next, final):
    t, d = x.shape
    dp = p.shape[1]
    tm = ROW_TILE
    row = lambda i: (i, 0)
    fixed = lambda i: (0, 0)
    pending_ffn = g_ple is not None
    in_specs = [pl.BlockSpec((tm, d), row)]
    args = [h]
    if pending_ffn:
        in_specs.append(pl.BlockSpec((1, d), fixed))
        args.append(g_ple.reshape(1, d))
    in_specs += [pl.BlockSpec((tm, dp), row), pl.BlockSpec((d, d), fixed), pl.BlockSpec((dp, d), fixed),
                 pl.BlockSpec((tm, d), row), pl.BlockSpec((1, d), fixed)]
    args += [p, wg, wp, x, g_next.reshape(1, d)]
    if final:
        out_specs = [pl.BlockSpec((tm, d), row)]
        out_shape = [jax.ShapeDtypeStruct((t, d), F32)]
    else:
        out_specs = [pl.BlockSpec((tm, d), row), pl.BlockSpec((tm, d), row)]
        out_shape = [jax.ShapeDtypeStruct((t, d), F32), jax.ShapeDtypeStruct((t, d), BF16)]
    return pl.pallas_call(
        functools.partial(_ple_kernel, pending_ffn, final),
        grid=(t // tm,), in_specs=in_specs, out_specs=out_specs, out_shape=out_shape,
        compiler_params=_params("arbitrary"), name="ple",
    )(*args)


def _pack_w_in(w):
    sizes = (M_WIDTH, M_WIDTH, M_WIDTH, M_WIDTH, M_HEADS, M_HEADS, S_WIDTH, S_XBC, S_HEADS,
             A_WIDTH, A_WIDTH, A_WIDTH)
    offs = [0]
    for s in sizes:
        offs.append(offs[-1] + s)
    seg = [w[:, offs[i]:offs[i + 1]] for i in range(len(sizes))]
    mq, mk, mv, mo, mi, mf, sz, sxbc, sdt, aq, ak, av = seg
    pad = jnp.zeros((w.shape[0], LANES - 2 * M_HEADS - S_HEADS), w.dtype)
    main = jnp.concatenate([mq, mk, mv, mo, sz, sxbc, aq, ak, av], axis=1).astype(BF16)
    gates = jnp.concatenate([mi, mf, sdt, pad], axis=1).astype(BF16)
    return main, gates


def _gate_lanes(vec, lane0):
    return jnp.zeros((1, LANES), F32).at[0, lane0:lane0 + vec.shape[0]].set(vec.astype(F32))


def kernel(x, p, ln_mix, w_in, w_out, m_conv_w, m_conv_b, m_gate_b, m_norm_g, s_conv_w, s_conv_b, s_dt_bias, s_a_log, s_d, s_norm_g, ln_ffn, ffn_w_gate, ffn_w_up, ffn_w_down, moe_router, moe_w_gate, moe_w_up, moe_w_down, ln_ple, ple_proj, ple_gate, ln_final):
    batch, seq, d = x.shape
    depth = w_in.shape[0]
    t = batch * seq
    xf = x.reshape(t, d)
    qk_scale = jnp.concatenate([jnp.ones((M_WIDTH,), F32),
                                jnp.full((M_WIDTH,), M_HEAD_DIM ** -0.5, F32)])
    xbc_scale = jnp.ones((S_XBC,), F32)
    xn = rmsnorm_bf16(xf, ln_mix[0])
    out = None
    for i in range(depth):
        w_main, w_gates = _pack_w_in(w_in[i])
        u = in_proj(xn, w_main, BF16, PROJ_COL_TILE)
        gates = in_proj(xn, w_gates, F32, LANES)
        qk = conv_silu(u, COL_MQ, 2 * M_WIDTH, m_conv_w[i], m_conv_b[i], qk_scale, batch, seq)
        xbc = conv_silu(u, COL_SXBC, S_XBC, s_conv_w[i], s_conv_b[i], xbc_scale, batch, seq)
        y_m = mlstm_mixer(u, gates, qk, _gate_lanes(m_gate_b[i], GATE_LANE_MI), m_norm_g[i], batch, seq)
        y_s = ssd_mixer(u, gates, xbc, _gate_lanes(s_dt_bias[i], GATE_LANE_SDT),
                        _gate_lanes(s_a_log[i], GATE_LANE_SDT),
                        jnp.repeat(s_d[i].astype(F32), S_HEAD_DIM), s_norm_g[i], batch, seq)
        y_a = moba_mixer(u, batch, seq)
        w_o = w_out[i].astype(BF16)
        j = i // 2
        if i % 2 == 0:
            xf, hn = out_proj(y_m, y_s, y_a, xf, w_o, ln_ffn[i])
            hp = dense_ffn(hn, ffn_w_gate[j].astype(BF16), ffn_w_up[j].astype(BF16),
                           ffn_w_down[j].astype(BF16))
            g_ple = ln_ple[i]
        else:
            w_r = jnp.zeros((d, LANES), F32).at[:, :N_EXPERTS].set(moe_router[j].astype(F32))
            w_r_hi = w_r.astype(BF16)
            w_r = jnp.stack([w_r_hi, (w_r - w_r_hi.astype(F32)).astype(BF16)])
            xf, hn, route = out_proj(y_m, y_s, y_a, xf, w_o, ln_ffn[i], w_r)
            xf, hp = moe_ffn(hn, route, xf, moe_w_gate[j].astype(BF16), moe_w_up[j].astype(BF16),
                             moe_w_down[j].astype(BF16), ln_ple[i])
            g_ple = None
        final = i == depth - 1
        g_next = ln_final if final else ln_mix[i + 1]
        res = ple(hp, g_ple, p[i].reshape(t, -1), ple_gate[i].astype(BF16), ple_proj[i].astype(BF16),
                  xf, g_next, final)
        if final:
            out = res[0]
        else:
            xf, xn = res
    return out.reshape(batch, seq, d)
```

```python
import functools

import jax
import jax.numpy as jnp
from jax import lax
from jax.experimental import pallas as pl
from jax.experimental.pallas import tpu as pltpu

F32 = jnp.float32
BF16 = jnp.bfloat16
HIGHEST = lax.Precision.HIGHEST
NEG_INF = float("-inf")

EPS = 1e-6
LANES = 128
VMEM_LIMIT = 56 * 1024 * 1024

D_MODEL = 2048
M_HEADS = 4
M_HEAD_DIM = 128
M_WIDTH = M_HEADS * M_HEAD_DIM
S_HEADS = 16
S_HEAD_DIM = 64
S_WIDTH = S_HEADS * S_HEAD_DIM
S_GROUPS = 2
S_STATE = 128
S_XBC = S_WIDTH + 2 * S_GROUPS * S_STATE
S_GROUP_WIDTH = S_WIDTH // S_GROUPS
A_HEADS = 4
A_HEAD_DIM = 128
A_WIDTH = A_HEADS * A_HEAD_DIM
A_BLOCK = 256
A_TOPK = 3
N_EXPERTS = 8
CONV_K = 4

COL_MQ = 0
COL_MK = COL_MQ + M_WIDTH
COL_MV = COL_MK + M_WIDTH
COL_MO = COL_MV + M_WIDTH
COL_SZ = COL_MO + M_WIDTH
COL_SXBC = COL_SZ + S_WIDTH
COL_AQ = COL_SXBC + S_XBC
COL_AK = COL_AQ + A_WIDTH
COL_AV = COL_AK + A_WIDTH
D_PROJ_PACKED = COL_AV + A_WIDTH
GATE_LANE_MI = 0
GATE_LANE_MF = M_HEADS
GATE_LANE_SDT = 2 * M_HEADS

ROW_TILE = 512
PROJ_ROW_TILE = 1024
PROJ_COL_TILE = 1536
FFN_ROW_TILE = 1024
FF_TILE = 512
MOE_FF_TILE = 1024
M_CHUNK = 256
S_CHUNK = 128
MOE_ROW_TILE = 512
GATHER_TILE = 256
DMA_UNROLL = 8


def _params(*sem):
    return pltpu.CompilerParams(dimension_semantics=sem, vmem_limit_bytes=VMEM_LIMIT)


def _rms(x, g):
    return x * lax.rsqrt(jnp.mean(x * x, axis=-1, keepdims=True) + EPS) * g


def _dot(a, b, precision=None):
    return jnp.dot(a, b, preferred_element_type=F32, precision=precision)


def _dot_nt(a, b, precision=None):
    return lax.dot_general(a, b, (((1,), (1,)), ((), ())), preferred_element_type=F32,
                           precision=precision)


def _softplus(x):
    return jnp.maximum(x, 0.0) + jnp.log1p(jnp.exp(-jnp.abs(x)))


def _silu(x):
    return x * jax.nn.sigmoid(x)


def _rmsnorm_kernel(x_ref, g_ref, o_ref):
    o_ref[...] = _rms(x_ref[...], g_ref[...]).astype(o_ref.dtype)


def rmsnorm_bf16(x, g):
    t, d = x.shape
    return pl.pallas_call(
        _rmsnorm_kernel,
        grid=(t // ROW_TILE,),
        in_specs=[pl.BlockSpec((ROW_TILE, d), lambda i: (i, 0)),
                  pl.BlockSpec((1, d), lambda i: (0, 0))],
        out_specs=pl.BlockSpec((ROW_TILE, d), lambda i: (i, 0)),
        out_shape=jax.ShapeDtypeStruct((t, d), BF16),
        compiler_params=_params("arbitrary"),
        name="rmsnorm",
    )(x, g.reshape(1, d))


def _proj_kernel(x_ref, w_ref, o_ref):
    o_ref[...] = _dot(x_ref[...], w_ref[...]).astype(o_ref.dtype)


def in_proj(xn, w, out_dtype, col_tile):
    t, d = xn.shape
    n = w.shape[1]
    tm = PROJ_ROW_TILE
    return pl.pallas_call(
        _proj_kernel,
        grid=(n // col_tile, t // tm),
        in_specs=[pl.BlockSpec((tm, d), lambda j, i: (i, 0)),
                  pl.BlockSpec((d, col_tile), lambda j, i: (0, j))],
        out_specs=pl.BlockSpec((tm, col_tile), lambda j, i: (i, j)),
        out_shape=jax.ShapeDtypeStruct((t, n), out_dtype),
        compiler_params=_params("arbitrary", "arbitrary"),
        name="in_proj",
    )(xn, w)


def _conv_silu_kernel(u_ref, w_ref, b_ref, s_ref, o_ref):
    x = u_ref[...].astype(F32)
    row = lax.broadcasted_iota(jnp.int32, x.shape, 0)
    y = x * w_ref[CONV_K - 1:CONV_K, :] + b_ref[...]
    for shift in range(1, CONV_K):
        xs = jnp.where(row >= shift, pltpu.roll(x, shift, axis=0), 0.0)
        y = y + xs * w_ref[CONV_K - 1 - shift:CONV_K - shift, :]
    o_ref[...] = (_silu(y) * s_ref[...]).astype(o_ref.dtype)


def conv_silu(u, col0, width, w, b, post_scale, batch, seq):
    ct = 512
    cb0 = col0 // ct
    return pl.pallas_call(
        _conv_silu_kernel,
        grid=(batch, width // ct),
        in_specs=[pl.BlockSpec((seq, ct), lambda b_, c: (b_, cb0 + c)),
                  pl.BlockSpec((CONV_K, ct), lambda b_, c: (0, c)),
                  pl.BlockSpec((1, ct), lambda b_, c: (0, c)),
                  pl.BlockSpec((1, ct), lambda b_, c: (0, c))],
        out_specs=pl.BlockSpec((seq, ct), lambda b_, c: (b_, c)),
        out_shape=jax.ShapeDtypeStruct((batch * seq, width), BF16),
        compiler_params=_params("arbitrary", "arbitrary"),
        name="conv_silu",
    )(u, w, b.reshape(1, width), post_scale.reshape(1, width))


def _mlstm_kernel(q_ref, k_ref, v_ref, o_ref, g_ref, gb_ref, ng_ref, y_ref, ct_ref, n_ref, m_ref):
    c = pl.program_id(1)
    L = q_ref.shape[0]
    H, Dh = M_HEADS, M_HEAD_DIM

    @pl.when(c == 0)
    def _():
        ct_ref[...] = jnp.zeros_like(ct_ref)
        n_ref[...] = jnp.zeros_like(n_ref)
        m_ref[...] = jnp.zeros_like(m_ref)

    g = g_ref[...] + gb_ref[...]
    lf = jnp.minimum(g, 0.0) - jnp.log1p(jnp.exp(-jnp.abs(g)))
    row = lax.broadcasted_iota(jnp.int32, (L, L), 0)
    col = lax.broadcasted_iota(jnp.int32, (L, L), 1)
    causal = row >= col
    tril = jnp.where(causal, 1.0, 0.0).astype(F32)
    bcum = _dot(tril, lf, HIGHEST)
    g_t = g.T
    bcum_t = bcum.T
    for h in range(H):
        li_col = g[:, GATE_LANE_MI + h:GATE_LANE_MI + h + 1]
        li_row = g_t[GATE_LANE_MI + h:GATE_LANE_MI + h + 1, :]
        b_col = bcum[:, GATE_LANE_MF + h:GATE_LANE_MF + h + 1]
        b_row = bcum_t[GATE_LANE_MF + h:GATE_LANE_MF + h + 1, :]
        m_prev = m_ref[h:h + 1, 0:1]
        sl = slice(h * Dh, (h + 1) * Dh)
        qh = q_ref[:, sl]
        kh = k_ref[:, sl]
        vh = v_ref[:, sl]

        logw = jnp.where(causal, b_col - b_row + li_row, NEG_INF)
        m_inter = b_col + m_prev
        m_j = jnp.maximum(m_inter, jnp.max(logw, axis=1, keepdims=True))
        w = jnp.exp(logw - m_j)
        s_inter = jnp.exp(m_inter - m_j)
        sqk = _dot_nt(qh, kh) * w
        num = s_inter * _dot(qh, ct_ref[h].astype(BF16)) + _dot(sqk.astype(BF16), vh)
        n_row = n_ref[h][0:1, :]
        den = (s_inter * jnp.sum(qh.astype(F32) * n_row, axis=1, keepdims=True)
               + jnp.sum(sqk, axis=1, keepdims=True))
        hh = num / jnp.maximum(jnp.abs(den), jnp.exp(-m_j))

        total = b_col[L - 1:L, :]
        lw_end_row = total - b_row + li_row
        lw_end_col = total - b_col + li_col
        m_new = jnp.maximum(total + m_prev, jnp.max(lw_end_row, axis=1, keepdims=True))
        a_row = jnp.exp(lw_end_row - m_new)
        a_col = jnp.exp(lw_end_col - m_new)
        decay = jnp.exp(total + m_prev - m_new)
        ka_t = (kh.astype(F32) * a_col).T.astype(BF16)
        ct_ref[h] = decay * ct_ref[h] + _dot(ka_t, vh)
        a8 = jnp.broadcast_to(a_row, (8, L)).astype(BF16)
        n_ref[h] = decay * n_ref[h] + _dot(a8, kh)
        m_ref[h:h + 1, :] = jnp.broadcast_to(m_new, (1, LANES))

        hn = hh * lax.rsqrt(jnp.mean(hh * hh, axis=-1, keepdims=True) + EPS)
        o_gate = jax.nn.sigmoid(o_ref[:, sl].astype(F32))
        y_ref[:, sl] = (hn * ng_ref[:, sl] * o_gate).astype(y_ref.dtype)


def mlstm_mixer(u, gates, qk, gate_b, norm_g, batch, seq):
    L = M_CHUNK
    nc = seq // L
    w = M_WIDTH
    rows = lambda b_, c: b_ * nc + c
    return pl.pallas_call(
        _mlstm_kernel,
        grid=(batch, nc),
        in_specs=[pl.BlockSpec((L, w), lambda b_, c: (rows(b_, c), 0)),
                  pl.BlockSpec((L, w), lambda b_, c: (rows(b_, c), 1)),
                  pl.BlockSpec((L, w), lambda b_, c: (rows(b_, c), COL_MV // w)),
                  pl.BlockSpec((L, w), lambda b_, c: (rows(b_, c), COL_MO // w)),
                  pl.BlockSpec((L, LANES), lambda b_, c: (rows(b_, c), 0)),
                  pl.BlockSpec((1, LANES), lambda b_, c: (0, 0)),
                  pl.BlockSpec((1, w), lambda b_, c: (0, 0))],
        out_specs=pl.BlockSpec((L, w), lambda b_, c: (rows(b_, c), 0)),
        out_shape=jax.ShapeDtypeStruct((batch * seq, w), BF16),
        scratch_shapes=[pltpu.VMEM((M_HEADS, M_HEAD_DIM, M_HEAD_DIM), F32),
                        pltpu.VMEM((M_HEADS, 8, M_HEAD_DIM), F32),
                        pltpu.VMEM((8, LANES), F32)],
        compiler_params=_params("arbitrary", "arbitrary"),
        name="mlstm",
    )(qk, qk, u, u, gates, gate_b, norm_g.reshape(1, w))


def _ssd_kernel(xbc_ref, z_ref, g_ref, dtb_ref, alog_ref, dskip_ref, ng_ref, y_ref, s_ref):
    c = pl.program_id(1)
    L = xbc_ref.shape[0]
    GW, N, P = S_GROUP_WIDTH, S_STATE, S_HEAD_DIM
    E = S_HEADS // S_GROUPS

    @pl.when(c == 0)
    def _():
        s_ref[...] = jnp.zeros_like(s_ref)

    dt = _softplus(g_ref[...] + dtb_ref[...])
    a = dt * (-jnp.exp(alog_ref[...]))
    row = lax.broadcasted_iota(jnp.int32, (L, L), 0)
    col = lax.broadcasted_iota(jnp.int32, (L, L), 1)
    causal = row >= col
    tril = jnp.where(causal, 1.0, 0.0).astype(F32)
    acum = _dot(tril, a, HIGHEST)
    acum_t = acum.T
    lane_p = lax.broadcasted_iota(jnp.int32, (L, LANES), 1)
    heads_per_slab = LANES // P
    for gi in range(S_GROUPS):
        gs = slice(gi * GW, (gi + 1) * GW)
        x = xbc_ref[:, gs].astype(F32)
        bm = xbc_ref[:, S_WIDTH + gi * N:S_WIDTH + (gi + 1) * N]
        cm = xbc_ref[:, S_WIDTH + S_GROUPS * N + gi * N:S_WIDTH + S_GROUPS * N + (gi + 1) * N]
        cb = _dot_nt(cm, bm)
        pieces, xdt_slabs, ac_slabs = [], [], []
        for pi in range(GW // LANES):
            x_slab = x[:, pi * LANES:(pi + 1) * LANES]
            dt_slab = ac_slab = None
            per_head = []
            for half in range(heads_per_slab):
                ln = GATE_LANE_SDT + gi * E + pi * heads_per_slab + half
                ac_col = jnp.broadcast_to(acum[:, ln:ln + 1], (L, LANES))
                dt_col = jnp.broadcast_to(dt[:, ln:ln + 1], (L, LANES))
                in_half = lane_p >= half * P
                dt_slab = dt_col if dt_slab is None else jnp.where(in_half, dt_col, dt_slab)
                ac_slab = ac_col if ac_slab is None else jnp.where(in_half, ac_col, ac_slab)
                seg = jnp.broadcast_to(acum[:, ln:ln + 1], (L, L)) - acum_t[ln:ln + 1, :]
                per_head.append(jnp.exp(jnp.where(causal, seg, NEG_INF)))
            xdt_slab = x_slab * dt_slab
            acc = None
            for half in range(heads_per_slab):
                in_half = (lane_p >= half * P) & (lane_p < (half + 1) * P)
                d = _dot((cb * per_head[half]).astype(BF16),
                         jnp.where(in_half, xdt_slab, 0.0).astype(BF16))
                acc = d if acc is None else acc + d
            pieces.append(acc)
            xdt_slabs.append(xdt_slab)
            ac_slabs.append(ac_slab)
        y = jnp.concatenate(pieces, axis=1)
        xdt = jnp.concatenate(xdt_slabs, axis=1)
        ac_x = jnp.concatenate(ac_slabs, axis=1)
        tot_x = ac_x[L - 1:L, :]
        state = s_ref[gi]
        y = y + _dot(cm, state.astype(BF16)) * jnp.exp(ac_x) + x * dskip_ref[:, gs]
        xdd = (xdt * jnp.exp(tot_x - ac_x)).astype(BF16)
        bm_t = bm.astype(F32).T.astype(BF16)
        s_ref[gi] = state * jnp.exp(tot_x) + _dot(bm_t, xdd)
        y = y * _silu(z_ref[:, gs].astype(F32))
        y_ref[:, gs] = _rms(y, ng_ref[:, gs]).astype(y_ref.dtype)


def ssd_mixer(u, gates, xbc, dt_bias, a_log, d_skip, norm_g, batch, seq):
    L = S_CHUNK
    nc = seq // L
    rows = lambda b_, c: b_ * nc + c
    return pl.pallas_call(
        _ssd_kernel,
        grid=(batch, nc),
        in_specs=[pl.BlockSpec((L, S_XBC), lambda b_, c: (rows(b_, c), 0)),
                  pl.BlockSpec((L, S_WIDTH), lambda b_, c: (rows(b_, c), COL_SZ // S_WIDTH)),
                  pl.BlockSpec((L, LANES), lambda b_, c: (rows(b_, c), 0)),
                  pl.BlockSpec((1, LANES), lambda b_, c: (0, 0)),
                  pl.BlockSpec((1, LANES), lambda b_, c: (0, 0)),
                  pl.BlockSpec((1, S_WIDTH), lambda b_, c: (0, 0)),
                  pl.BlockSpec((1, S_WIDTH), lambda b_, c: (0, 0))],
        out_specs=pl.BlockSpec((L, S_WIDTH), lambda b_, c: (rows(b_, c), 0)),
        out_shape=jax.ShapeDtypeStruct((batch * seq, S_WIDTH), BF16),
        scratch_shapes=[pltpu.VMEM((S_GROUPS, S_STATE, S_GROUP_WIDTH), F32)],
        compiler_params=_params("arbitrary", "arbitrary"),
        name="ssd",
    )(xbc, u, gates, dt_bias, a_log, d_skip.reshape(1, S_WIDTH), norm_g.reshape(1, S_WIDTH))


def _moba_kernel(q_ref, k_ref, v_ref, ex_ref, y_ref, kmean_ref):
    qi = pl.program_id(1)
    BS = A_BLOCK
    S = k_ref.shape[0]
    nb = S // BS
    nbp = kmean_ref.shape[1]
    H, Dh = A_HEADS, A_HEAD_DIM
    heads = [slice(h * Dh, (h + 1) * Dh) for h in range(H)]

    @pl.when(qi == 0)
    def _():
        for h in range(H):
            k = k_ref[:, heads[h]].astype(F32)
            k_mean = jnp.sum(k.reshape(nb, BS, Dh), axis=1) * (1.0 / BS)
            if nbp > nb:
                k_mean = jnp.concatenate([k_mean, jnp.zeros((nbp - nb, Dh), F32)], axis=0)
            kmean_ref[h] = k_mean

    blk = lax.broadcasted_iota(jnp.int32, (nbp, BS), 0)
    qbs, sels = [], []
    for h in range(H):
        q = q_ref[:, heads[h]].astype(F32) * (Dh ** -0.5)
        bs_t = _dot_nt(kmean_ref[h], q, HIGHEST)
        cnt = jnp.zeros((nbp, BS), F32)
        for j in range(nb):
            cj = bs_t[j:j + 1, :]
            beats = jnp.where(cj > bs_t, 1.0, jnp.where((cj == bs_t) & (blk > j), 1.0, 0.0))
            cnt = cnt + jnp.where(qi > j, beats, 0.0)
        sel_t = jnp.where((blk < qi) & (cnt < A_TOPK), 1.0, 0.0)
        sels.append(jnp.concatenate([sel_t, jnp.zeros((LANES - nbp, BS), F32)], axis=0).T.astype(BF16))
        qbs.append(q.astype(BF16))

    row = lax.broadcasted_iota(jnp.int32, (BS, BS), 0)
    col = lax.broadcasted_iota(jnp.int32, (BS, BS), 1)
    own = jnp.where(row >= col, 1.0, 0.0)
    for n in range(1, nb + 1):
        @pl.when(qi == n - 1)
        def _(n=n):
            w = n * BS
            for h in range(H):
                s = _dot_nt(qbs[h], k_ref[0:w, heads[h]])
                if n > 1:
                    allowed = jnp.concatenate([_dot(sels[h], ex_ref[:, 0:w - BS]), own], axis=1)
                else:
                    allowed = own
                s = jnp.where(allowed > 0.5, s, NEG_INF)
                m = jnp.max(s, axis=1, keepdims=True)
                p = jnp.exp(s - m)
                l = jnp.sum(p, axis=1, keepdims=True)
                y_ref[:, heads[h]] = (_dot(p.astype(BF16), v_ref[0:w, heads[h]]) / l).astype(y_ref.dtype)


def moba_mixer(u, batch, seq):
    nq = seq // A_BLOCK
    nbp = -(-nq // 8) * 8
    w = A_WIDTH
    expand = (jnp.arange(LANES, dtype=jnp.int32)[:, None]
              == jnp.arange(seq, dtype=jnp.int32)[None, :] // A_BLOCK).astype(BF16)
    return pl.pallas_call(
        _moba_kernel,
        grid=(batch, nq),
        in_specs=[pl.BlockSpec((A_BLOCK, w), lambda b_, i: (b_ * nq + i, COL_AQ // w)),
                  pl.BlockSpec((seq, w), lambda b_, i: (b_, COL_AK // w)),
                  pl.BlockSpec((seq, w), lambda b_, i: (b_, COL_AV // w)),
                  pl.BlockSpec((LANES, seq), lambda b_, i: (0, 0))],
        out_specs=pl.BlockSpec((A_BLOCK, w), lambda b_, i: (b_ * nq + i, 0)),
        out_shape=jax.ShapeDtypeStruct((batch * seq, w), BF16),
        scratch_shapes=[pltpu.VMEM((A_HEADS, nbp, A_HEAD_DIM), F32)],
        compiler_params=_params("arbitrary", "arbitrary"),
        name="moba",
    )(u, u, u, expand)


def _out_proj_body(ym_ref, ys_ref, ya_ref, x_ref, w_ref, g_ref):
    x = x_ref[...]
    x = x + _dot(ym_ref[...], w_ref[0:M_WIDTH, :])
    x = x + _dot(ys_ref[...], w_ref[M_WIDTH:M_WIDTH + S_WIDTH, :])
    x = x + _dot(ya_ref[...], w_ref[M_WIDTH + S_WIDTH:, :])
    return x, _rms(x, g_ref[...])


def _out_proj_kernel(ym_ref, ys_ref, ya_ref, x_ref, w_ref, g_ref, xo_ref, hn_ref):
    x, hn = _out_proj_body(ym_ref, ys_ref, ya_ref, x_ref, w_ref, g_ref)
    xo_ref[...] = x
    hn_ref[...] = hn.astype(hn_ref.dtype)


def _out_proj_route_kernel(ym_ref, ys_ref, ya_ref, x_ref, w_ref, g_ref, wr_ref, xo_ref, hn_ref, route_ref):
    x, hn = _out_proj_body(ym_ref, ys_ref, ya_ref, x_ref, w_ref, g_ref)
    xo_ref[...] = x
    hn_ref[...] = hn
    hn_hi = hn.astype(BF16)
    hn_lo = (hn - hn_hi.astype(F32)).astype(BF16)
    logits = (_dot(hn_hi, wr_ref[0]) + _dot(hn_lo, wr_ref[0])
              + _dot(hn_hi, wr_ref[1]))
    lane = lax.broadcasted_iota(jnp.int32, logits.shape, 1)
    logits = jnp.where(lane < N_EXPERTS, logits, NEG_INF)
    m1 = jnp.max(logits, axis=1, keepdims=True)
    i1 = jnp.min(jnp.where(logits == m1, lane, LANES), axis=1, keepdims=True)
    rest = jnp.where(lane == i1, NEG_INF, logits)
    m2 = jnp.max(rest, axis=1, keepdims=True)
    i2 = jnp.min(jnp.where(rest == m2, lane, LANES), axis=1, keepdims=True)
    e2 = jnp.exp(m2 - m1)
    w1 = 1.0 / (1.0 + e2)
    w2 = e2 / (1.0 + e2)
    route_ref[...] = jnp.where(lane == 0, i1.astype(F32),
                               jnp.where(lane == 1, i2.astype(F32),
                                         jnp.where(lane == 2, w1, jnp.where(lane == 3, w2, 0.0))))


def out_proj(ym, ys, ya, x, w, g, w_router=None):
    t, d = x.shape
    tm = ROW_TILE
    row = lambda i: (i, 0)
    fixed = lambda i: (0, 0)
    in_specs = [pl.BlockSpec((tm, M_WIDTH), row), pl.BlockSpec((tm, S_WIDTH), row),
                pl.BlockSpec((tm, A_WIDTH), row), pl.BlockSpec((tm, d), row),
                pl.BlockSpec((d, d), fixed), pl.BlockSpec((1, d), fixed)]
    args = [ym, ys, ya, x, w, g.reshape(1, d)]
    if w_router is None:
        kern = _out_proj_kernel
        out_specs = [pl.BlockSpec((tm, d), row), pl.BlockSpec((tm, d), row)]
        out_shape = [jax.ShapeDtypeStruct((t, d), F32), jax.ShapeDtypeStruct((t, d), BF16)]
    else:
        kern = _out_proj_route_kernel
        in_specs.append(pl.BlockSpec((2, d, LANES), lambda i: (0, 0, 0)))
        args.append(w_router)
        out_specs = [pl.BlockSpec((tm, d), row), pl.BlockSpec((tm, d), row),
                     pl.BlockSpec((tm, LANES), row)]
        out_shape = [jax.ShapeDtypeStruct((t, d), F32), jax.ShapeDtypeStruct((t, d), F32),
                     jax.ShapeDtypeStruct((t, LANES), F32)]
    return pl.pallas_call(
        kern, grid=(t // tm,), in_specs=in_specs, out_specs=out_specs, out_shape=out_shape,
        compiler_params=_params("arbitrary"), name="out_proj",
    )(*args)


def _ffn_kernel(hn_ref, wg_ref, wu_ref, wd_ref, o_ref):
    @pl.when(pl.program_id(1) == 0)
    def _():
        o_ref[...] = jnp.zeros_like(o_ref)

    hn = hn_ref[...]
    h = _silu(_dot(hn, wg_ref[...])) * _dot(hn, wu_ref[...])
    o_ref[...] += _dot(h.astype(BF16), wd_ref[...])


def dense_ffn(hn, wg, wu, wd):
    t, d = hn.shape
    ff = wg.shape[1]
    tm, tf = FFN_ROW_TILE, FF_TILE
    return pl.pallas_call(
        _ffn_kernel,
        grid=(t // tm, ff // tf),
        in_specs=[pl.BlockSpec((tm, d), lambda i, f: (i, 0)),
                  pl.BlockSpec((d, tf), lambda i, f: (0, f)),
                  pl.BlockSpec((d, tf), lambda i, f: (0, f)),
                  pl.BlockSpec((tf, d), lambda i, f: (f, 0))],
        out_specs=pl.BlockSpec((tm, d), lambda i, f: (i, 0)),
        out_shape=jax.ShapeDtypeStruct((t, d), F32),
        compiler_params=_params("arbitrary", "arbitrary"),
        name="dense_ffn",
    )(hn, wg, wu, wd)


def _expert_kernel(src_ref, te_ref, nu_ref, h_hbm, wg_ref, wu_ref, wd_ref, y_ref, gbuf_ref, xs_ref, sem):
    i = pl.program_id(0)
    f = pl.program_id(1)
    tm = y_ref.shape[0]
    n_used = nu_ref[0]
    used = i < n_used
    slot = lax.rem(i, 2)

    def row_copy(tile, r, slot_):
        return pltpu.make_async_copy(h_hbm.at[pl.ds(src_ref[tile * tm + r], 1)],
                                     gbuf_ref.at[slot_, pl.ds(r, 1)], sem.at[slot_])

    def start_tile(tile, slot_):
        def body(r, carry):
            row_copy(tile, r, slot_).start()
            return carry
        lax.fori_loop(0, tm, body, 0, unroll=DMA_UNROLL)

    def wait_tile(tile, slot_):
        def body(r, carry):
            row_copy(tile, r, slot_).wait()
            return carry
        lax.fori_loop(0, tm, body, 0, unroll=DMA_UNROLL)

    @pl.when(f == 0)
    def _():
        y_ref[...] = jnp.zeros_like(y_ref)

    @pl.when((f == 0) & used)
    def _():
        @pl.when(i == 0)
        def _():
            start_tile(0, 0)

        wait_tile(i, slot)
        xs_ref[...] = gbuf_ref[slot].astype(xs_ref.dtype)

        @pl.when(i + 1 < n_used)
        def _():
            start_tile(i + 1, 1 - slot)

    @pl.when(used)
    def _():
        xs = xs_ref[...]
        h = _silu(_dot(xs, wg_ref[0])) * _dot(xs, wu_ref[0])
        y_ref[...] += _dot(h.astype(BF16), wd_ref[0])


def expert_ffn(h, src, wg, wu, wd, tile_expert, n_used):
    t, d = h.shape
    r = src.shape[0]
    ff = wg.shape[2]
    tm, tf = MOE_ROW_TILE, MOE_FF_TILE
    nf = ff // tf

    def fblk(i, f, nu):
        return jnp.where(i < nu[0], f, nf - 1)

    return pl.pallas_call(
        _expert_kernel,
        grid_spec=pltpu.PrefetchScalarGridSpec(
            num_scalar_prefetch=3,
            grid=(r // tm, nf),
            in_specs=[pl.BlockSpec(memory_space=pl.ANY),
                      pl.BlockSpec((1, d, tf), lambda i, f, s, te, nu: (te[i], 0, fblk(i, f, nu))),
                      pl.BlockSpec((1, d, tf), lambda i, f, s, te, nu: (te[i], 0, fblk(i, f, nu))),
                      pl.BlockSpec((1, tf, d), lambda i, f, s, te, nu: (te[i], fblk(i, f, nu), 0))],
            out_specs=pl.BlockSpec((tm, d), lambda i, f, s, te, nu: (i, 0)),
            scratch_shapes=[pltpu.VMEM((2, tm, d), F32), pltpu.VMEM((tm, d), BF16),
                            pltpu.SemaphoreType.DMA((2,))]),
        out_shape=jax.ShapeDtypeStruct((r, d), F32),
        compiler_params=_params("arbitrary", "arbitrary"),
        name="moe_experts",
    )(src, tile_expert, n_used, h, wg, wu, wd)


def _combine_kernel(pos_ref, y_hbm, x_ref, route_ref, g_ref, xo_ref, hp_ref, buf_ref, sem):
    i = pl.program_id(0)
    n = pl.num_programs(0)
    tc = x_ref.shape[0]
    t_total = n * tc
    slot = lax.rem(i, 2)

    def copy(tile, r, choice, slot_):
        return pltpu.make_async_copy(y_hbm.at[pl.ds(pos_ref[choice * t_total + tile * tc + r], 1)],
                                     buf_ref.at[slot_, choice, pl.ds(r, 1)], sem.at[slot_])

    def start_tile(tile, slot_):
        def body(r, carry):
            copy(tile, r, 0, slot_).start()
            copy(tile, r, 1, slot_).start()
            return carry
        lax.fori_loop(0, tc, body, 0, unroll=DMA_UNROLL)

    def wait_tile(tile, slot_):
        def body(r, carry):
            copy(tile, r, 0, slot_).wait()
            copy(tile, r, 1, slot_).wait()
            return carry
        lax.fori_loop(0, tc, body, 0, unroll=DMA_UNROLL)

    @pl.when(i == 0)
    def _():
        start_tile(0, 0)

    @pl.when(i + 1 < n)
    def _():
        start_tile(i + 1, 1 - slot)

    wait_tile(i, slot)
    route = route_ref[...]
    x = x_ref[...] + route[:, 2:3] * buf_ref[slot, 0] + route[:, 3:4] * buf_ref[slot, 1]
    xo_ref[...] = x
    hp_ref[...] = _rms(x, g_ref[...]).astype(hp_ref.dtype)


def moe_combine(y, pos, x, route, g_next):
    t, d = x.shape
    tc = GATHER_TILE
    return pl.pallas_call(
        _combine_kernel,
        grid_spec=pltpu.PrefetchScalarGridSpec(
            num_scalar_prefetch=1,
            grid=(t // tc,),
            in_specs=[pl.BlockSpec(memory_space=pl.ANY),
                      pl.BlockSpec((tc, d), lambda i, p_: (i, 0)),
                      pl.BlockSpec((tc, LANES), lambda i, p_: (i, 0)),
                      pl.BlockSpec((1, d), lambda i, p_: (0, 0))],
            out_specs=[pl.BlockSpec((tc, d), lambda i, p_: (i, 0)),
                       pl.BlockSpec((tc, d), lambda i, p_: (i, 0))],
            scratch_shapes=[pltpu.VMEM((2, 2, tc, d), F32), pltpu.SemaphoreType.DMA((2,))]),
        out_shape=[jax.ShapeDtypeStruct((t, d), F32), jax.ShapeDtypeStruct((t, d), BF16)],
        compiler_params=_params("arbitrary"),
        name="moe_combine",
    )(pos, y, x, route, g_next.reshape(1, d))


def moe_ffn(hn, route, x, wg, wu, wd, g_next):
    t, d = x.shape
    tm = MOE_ROW_TILE
    n_tiles = (2 * t) // tm + N_EXPERTS
    e_flat = jnp.concatenate([route[:, 0], route[:, 1]]).astype(jnp.int32)
    onehot = (e_flat[:, None] == jnp.arange(N_EXPERTS, dtype=jnp.int32)[None, :]).astype(jnp.int32)
    counts = jnp.sum(onehot, axis=0)
    rank = jnp.sum((jnp.cumsum(onehot, axis=0) - onehot) * onehot, axis=1)
    padded = ((counts + tm - 1) // tm) * tm
    ends = jnp.cumsum(padded)
    offs = ends - padded
    pos = (jnp.sum(onehot * offs[None, :], axis=1) + rank).astype(jnp.int32)
    token = jnp.concatenate([jnp.arange(t, dtype=jnp.int32)] * 2)
    src = jnp.zeros((n_tiles * tm,), jnp.int32).at[pos].set(token)
    n_used = (ends[-1] // tm).astype(jnp.int32).reshape(1)
    tile_start = jnp.minimum(jnp.arange(n_tiles, dtype=jnp.int32), n_used[0] - 1) * tm
    tile_expert = jnp.sum((ends[None, :] <= tile_start[:, None]).astype(jnp.int32), axis=1)
    tile_expert = jnp.minimum(tile_expert, N_EXPERTS - 1).astype(jnp.int32)

    y = expert_ffn(hn, src, wg, wu, wd, tile_expert, n_used)
    return moe_combine(y, pos, x, route, g_next)


def _ple_kernel(pending_ffn, final, *refs):
    if pending_ffn:
        h_ref, gp_ref, p_ref, wg_ref, wp_ref, x_ref, g_ref, *outs = refs
        x = x_ref[...] + h_ref[...]
        hp = _rms(x, gp_ref[...]).astype(BF16)
    else:
        h_ref, p_ref, wg_ref, wp_ref, x_ref, g_ref, *outs = refs
        x = x_ref[...]
        hp = h_ref[...]
    gate = jax.nn.sigmoid(_dot(hp, wg_ref[...]))
    x = x + _dot(p_ref[...].astype(BF16), wp_ref[...]) * gate
    if final:
        outs[0][...] = _rms(x, g_ref[...])
    else:
        outs[0][...] = x
        outs[1][...] = _rms(x, g_ref[...]).astype(outs[1].dtype)


def ple(h, g_ple, p, wg, wp, x, g_next, final):
    t, d = x.shape
    dp = p.shape[1]
    tm = ROW_TILE
    row = lambda i: (i, 0)
    fixed = lambda i: (0, 0)
    pending_ffn = g_ple is not None
    in_specs = [pl.BlockSpec((tm, d), row)]
    args = [h]
    if pending_ffn:
        in_specs.append(pl.BlockSpec((1, d), fixed))
        args.append(g_ple.reshape(1, d))
    in_specs += [pl.BlockSpec((tm, dp), row), pl.BlockSpec((d, d), fixed), pl.BlockSpec((dp, d), fixed),
                 pl.BlockSpec((tm, d), row), pl.BlockSpec((1, d), fixed)]
    args += [p, wg, wp, x, g_next.reshape(1, d)]
    if final:
        out_specs = [pl.BlockSpec((tm, d), row)]
        out_shape = [jax.ShapeDtypeStruct((t, d), F32)]
    else:
        out_specs = [pl.BlockSpec((tm, d), row), pl.BlockSpec((tm, d), row)]
        out_shape = [jax.ShapeDtypeStruct((t, d), F32), jax.ShapeDtypeStruct((t, d), BF16)]
    return pl.pallas_call(
        functools.partial(_ple_kernel, pending_ffn, final),
        grid=(t // tm,), in_specs=in_specs, out_specs=out_specs, out_shape=out_shape,
        compiler_params=_params("arbitrary"), name="ple",
    )(*args)


def _pack_w_in(w):
    sizes = (M_WIDTH, M_WIDTH, M_WIDTH, M_WIDTH, M_HEADS, M_HEADS, S_WIDTH, S_XBC, S_HEADS,
             A_WIDTH, A_WIDTH, A_WIDTH)
    offs = [0]
    for s in sizes:
        offs.append(offs[-1] + s)
    seg = [w[:, offs[i]:offs[i + 1]] for i in range(len(sizes))]
    mq, mk, mv, mo, mi, mf, sz, sxbc, sdt, aq, ak, av = seg
    pad = jnp.zeros((w.shape[0], LANES - 2 * M_HEADS - S_HEADS), w.dtype)
    main = jnp.concatenate([mq, mk, mv, mo, sz, sxbc, aq, ak, av], axis=1).astype(BF16)
    gates = jnp.concatenate([mi, mf, sdt, pad], axis=1).astype(BF16)
    return main, gates


def _gate_lanes(vec, lane0):
    return jnp.zeros((1, LANES), F32).at[0, lane0:lane0 + vec.shape[0]].set(vec.astype(F32))


def kernel(x, p, ln_mix, w_in, w_out, m_conv_w, m_conv_b, m_gate_b, m_norm_g, s_conv_w, s_conv_b, s_dt_bias, s_a_log, s_d, s_norm_g, ln_ffn, ffn_w_gate, ffn_w_up, ffn_w_down, moe_router, moe_w_gate, moe_w_up, moe_w_down, ln_ple, ple_proj, ple_gate, ln_final):
    batch, seq, d = x.shape
    depth = w_in.shape[0]
    t = batch * seq
    xf = x.reshape(t, d)
    qk_scale = jnp.concatenate([jnp.ones((M_WIDTH,), F32),
                                jnp.full((M_WIDTH,), M_HEAD_DIM ** -0.5, F32)])
    xbc_scale = jnp.ones((S_XBC,), F32)
    xn = rmsnorm_bf16(xf, ln_mix[0])
    out = None
    for i in range(depth):
        w_main, w_gates = _pack_w_in(w_in[i])
        u = in_proj(xn, w_main, BF16, PROJ_COL_TILE)
        gates = in_proj(xn, w_gates, F32, LANES)
        qk = conv_silu(u, COL_MQ, 2 * M_WIDTH, m_conv_w[i], m_conv_b[i], qk_scale, batch, seq)
        xbc = conv_silu(u, COL_SXBC, S_XBC, s_conv_w[i], s_conv_b[i], xbc_scale, batch, seq)
        y_m = mlstm_mixer(u, gates, qk, _gate_lanes(m_gate_b[i], GATE_LANE_MI), m_norm_g[i], batch, seq)
        y_s = ssd_mixer(u, gates, xbc, _gate_lanes(s_dt_bias[i], GATE_LANE_SDT),
                        _gate_lanes(s_a_log[i], GATE_LANE_SDT),
                        jnp.repeat(s_d[i].astype(F32), S_HEAD_DIM), s_norm_g[i], batch, seq)
        y_a = moba_mixer(u, batch, seq)
        w_o = w_out[i].astype(BF16)
        j = i // 2
        if i % 2 == 0:
            xf, hn = out_proj(y_m, y_s, y_a, xf, w_o, ln_ffn[i])
            hp = dense_ffn(hn, ffn_w_gate[j].astype(BF16), ffn_w_up[j].astype(BF16),
                           ffn_w_down[j].astype(BF16))
            g_ple = ln_ple[i]
        else:
            w_r = jnp.zeros((d, LANES), F32).at[:, :N_EXPERTS].set(moe_router[j].astype(F32))
            w_r_hi = w_r.astype(BF16)
            w_r = jnp.stack([w_r_hi, (w_r - w_r_hi.astype(F32)).astype(BF16)])
            xf, hn, route = out_proj(y_m, y_s, y_a, xf, w_o, ln_ffn[i], w_r)
            xf, hp = moe_ffn(hn, route, xf, moe_w_gate[j].astype(BF16), moe_w_up[j].astype(BF16),
                             moe_w_down[j].astype(BF16), ln_ple[i])
            g_ple = None
        final = i == depth - 1
        g_next = ln_final if final else ln_mix[i + 1]
        res = ple(hp, g_ple, p[i].reshape(t, -1), ple_gate[i].astype(BF16), ple_proj[i].astype(BF16),
                  xf, g_next, final)
        if final:
            out = res[0]
        else:
            xf, xn = res
    return out.reshape(batch, seq, d)
```

```python
import functools

import jax
import jax.numpy as jnp
from jax import lax
from jax.experimental import pallas as pl
from jax.experimental.pallas import tpu as pltpu

F32 = jnp.float32
BF16 = jnp.bfloat16
HIGHEST = lax.Precision.HIGHEST
NEG_INF = float("-inf")

EPS = 1e-6
LANES = 128
VMEM_LIMIT = 56 * 1024 * 1024

D_MODEL = 2048
M_HEADS = 4
M_HEAD_DIM = 128
M_WIDTH = M_HEADS * M_HEAD_DIM
S_HEADS = 16
S_HEAD_DIM = 64
S_WIDTH = S_HEADS * S_HEAD_DIM
S_GROUPS = 2
S_STATE = 128
S_XBC = S_WIDTH + 2 * S_GROUPS * S_STATE
S_GROUP_WIDTH = S_WIDTH // S_GROUPS
A_HEADS = 4
A_HEAD_DIM = 128
A_WIDTH = A_HEADS * A_HEAD_DIM
A_BLOCK = 256
A_TOPK = 3
N_EXPERTS = 8
CONV_K = 4

COL_MQ = 0
COL_MK = COL_MQ + M_WIDTH
COL_MV = COL_MK + M_WIDTH
COL_MO = COL_MV + M_WIDTH
COL_SZ = COL_MO + M_WIDTH
COL_SXBC = COL_SZ + S_WIDTH
COL_AQ = COL_SXBC + S_XBC
COL_AK = COL_AQ + A_WIDTH
COL_AV = COL_AK + A_WIDTH
D_PROJ_PACKED = COL_AV + A_WIDTH
GATE_LANE_MI = 0
GATE_LANE_MF = M_HEADS
GATE_LANE_SDT = 2 * M_HEADS

ROW_TILE = 512
PROJ_ROW_TILE = 1024
PROJ_COL_TILE = 1536
FFN_ROW_TILE = 1024
FF_TILE = 512
MOE_FF_TILE = 1024
M_CHUNK = 256
S_CHUNK = 128
MOE_ROW_TILE = 512
GATHER_TILE = 256
DMA_UNROLL = 8


def _params(*sem):
    return pltpu.CompilerParams(dimension_semantics=sem, vmem_limit_bytes=VMEM_LIMIT)


def _rms(x, g):
    return x * lax.rsqrt(jnp.mean(x * x, axis=-1, keepdims=True) + EPS) * g


def _dot(a, b, precision=None):
    return jnp.dot(a, b, preferred_element_type=F32, precision=precision)


def _dot_nt(a, b, precision=None):
    return lax.dot_general(a, b, (((1,), (1,)), ((), ())), preferred_element_type=F32,
                           precision=precision)


def _softplus(x):
    return jnp.maximum(x, 0.0) + jnp.log1p(jnp.exp(-jnp.abs(x)))


def _silu(x):
    return x * jax.nn.sigmoid(x)


def _rmsnorm_kernel(x_ref, g_ref, o_ref):
    o_ref[...] = _rms(x_ref[...], g_ref[...]).astype(o_ref.dtype)


def rmsnorm_bf16(x, g):
    t, d = x.shape
    return pl.pallas_call(
        _rmsnorm_kernel,
        grid=(t // ROW_TILE,),
        in_specs=[pl.BlockSpec((ROW_TILE, d), lambda i: (i, 0)),
                  pl.BlockSpec((1, d), lambda i: (0, 0))],
        out_specs=pl.BlockSpec((ROW_TILE, d), lambda i: (i, 0)),
        out_shape=jax.ShapeDtypeStruct((t, d), BF16),
        compiler_params=_params("arbitrary"),
        name="rmsnorm",
    )(x, g.reshape(1, d))


def _proj_kernel(x_ref, w_ref, o_ref):
    o_ref[...] = _dot(x_ref[...], w_ref[...]).astype(o_ref.dtype)


def in_proj(xn, w, out_dtype, col_tile):
    t, d = xn.shape
    n = w.shape[1]
    tm = PROJ_ROW_TILE
    return pl.pallas_call(
        _proj_kernel,
        grid=(n // col_tile, t // tm),
        in_specs=[pl.BlockSpec((tm, d), lambda j, i: (i, 0)),
                  pl.BlockSpec((d, col_tile), lambda j, i: (0, j))],
        out_specs=pl.BlockSpec((tm, col_tile), lambda j, i: (i, j)),
        out_shape=jax.ShapeDtypeStruct((t, n), out_dtype),
        compiler_params=_params("arbitrary", "arbitrary"),
        name="in_proj",
    )(xn, w)


def _conv_silu_kernel(u_ref, w_ref, b_ref, s_ref, o_ref):
    x = u_ref[...].astype(F32)
    row = lax.broadcasted_iota(jnp.int32, x.shape, 0)
    y = x * w_ref[CONV_K - 1:CONV_K, :] + b_ref[...]
    for shift in range(1, CONV_K):
        xs = jnp.where(row >= shift, pltpu.roll(x, shift, axis=0), 0.0)
        y = y + xs * w_ref[CONV_K - 1 - shift:CONV_K - shift, :]
    o_ref[...] = (_silu(y) * s_ref[...]).astype(o_ref.dtype)


def conv_silu(u, col0, width, w, b, post_scale, batch, seq):
    ct = 512
    cb0 = col0 // ct
    return pl.pallas_call(
        _conv_silu_kernel,
        grid=(batch, width // ct),
        in_specs=[pl.BlockSpec((seq, ct), lambda b_, c: (b_, cb0 + c)),
                  pl.BlockSpec((CONV_K, ct), lambda b_, c: (0, c)),
                  pl.BlockSpec((1, ct), lambda b_, c: (0, c)),
                  pl.BlockSpec((1, ct), lambda b_, c: (0, c))],
        out_specs=pl.BlockSpec((seq, ct), lambda b_, c: (b_, c)),
        out_shape=jax.ShapeDtypeStruct((batch * seq, width), BF16),
        compiler_params=_params("arbitrary", "arbitrary"),
        name="conv_silu",
    )(u, w, b.reshape(1, width), post_scale.reshape(1, width))


def _mlstm_kernel(q_ref, k_ref, v_ref, o_ref, g_ref, gb_ref, ng_ref, y_ref, ct_ref, n_ref, m_ref):
    c = pl.program_id(1)
    L = q_ref.shape[0]
    H, Dh = M_HEADS, M_HEAD_DIM

    @pl.when(c == 0)
    def _():
        ct_ref[...] = jnp.zeros_like(ct_ref)
        n_ref[...] = jnp.zeros_like(n_ref)
        m_ref[...] = jnp.zeros_like(m_ref)

    g = g_ref[...] + gb_ref[...]
    lf = jnp.minimum(g, 0.0) - jnp.log1p(jnp.exp(-jnp.abs(g)))
    row = lax.broadcasted_iota(jnp.int32, (L, L), 0)
    col = lax.broadcasted_iota(jnp.int32, (L, L), 1)
    causal = row >= col
    tril = jnp.where(causal, 1.0, 0.0).astype(F32)
    bcum = _dot(tril, lf, HIGHEST)
    g_t = g.T
    bcum_t = bcum.T
    for h in range(H):
        li_col = g[:, GATE_LANE_MI + h:GATE_LANE_MI + h + 1]
        li_row = g_t[GATE_LANE_MI + h:GATE_LANE_MI + h + 1, :]
        b_col = bcum[:, GATE_LANE_MF + h:GATE_LANE_MF + h + 1]
        b_row = bcum_t[GATE_LANE_MF + h:GATE_LANE_MF + h + 1, :]
        m_prev = m_ref[h:h + 1, 0:1]
        sl = slice(h * Dh, (h + 1) * Dh)
        qh = q_ref[:, sl]
        kh = k_ref[:, sl]
        vh = v_ref[:, sl]

        logw = jnp.where(causal, b_col - b_row + li_row, NEG_INF)
        m_inter = b_col + m_prev
        m_j = jnp.maximum(m_inter, jnp.max(logw, axis=1, keepdims=True))
        w = jnp.exp(logw - m_j)
        s_inter = jnp.exp(m_inter - m_j)
        sqk = _dot_nt(qh, kh) * w
        num = s_inter * _dot(qh, ct_ref[h].astype(BF16)) + _dot(sqk.astype(BF16), vh)
        n_row = n_ref[h][0:1, :]
        den = (s_inter * jnp.sum(qh.astype(F32) * n_row, axis=1, keepdims=True)
               + jnp.sum(sqk, axis=1, keepdims=True))
        hh = num / jnp.maximum(jnp.abs(den), jnp.exp(-m_j))

        total = b_col[L - 1:L, :]
        lw_end_row = total - b_row + li_row
        lw_end_col = total - b_col + li_col
        m_new = jnp.maximum(total + m_prev, jnp.max(lw_end_row, axis=1, keepdims=True))
        a_row = jnp.exp(lw_end_row - m_new)
        a_col = jnp.exp(lw_end_col - m_new)
        decay = jnp.exp(total + m_prev - m_new)
        ka_t = (kh.astype(F32) * a_col).T.astype(BF16)
        ct_ref[h] = decay * ct_ref[h] + _dot(ka_t, vh)
        a8 = jnp.broadcast_to(a_row, (8, L)).astype(BF16)
        n_ref[h] = decay * n_ref[h] + _dot(a8, kh)
        m_ref[h:h + 1, :] = jnp.broadcast_to(m_new, (1, LANES))

        hn = hh * lax.rsqrt(jnp.mean(hh * hh, axis=-1, keepdims=True) + EPS)
        o_gate = jax.nn.sigmoid(o_ref[:, sl].astype(F32))
        y_ref[:, sl] = (hn * ng_ref[:, sl] * o_gate).astype(y_ref.dtype)


def mlstm_mixer(u, gates, qk, gate_b, norm_g, batch, seq):
    L = M_CHUNK
    nc = seq // L
    w = M_WIDTH
    rows = lambda b_, c: b_ * nc + c
    return pl.pallas_call(
        _mlstm_kernel,
        grid=(batch, nc),
        in_specs=[pl.BlockSpec((L, w), lambda b_, c: (rows(b_, c), 0)),
                  pl.BlockSpec((L, w), lambda b_, c: (rows(b_, c), 1)),
                  pl.BlockSpec((L, w), lambda b_, c: (rows(b_, c), COL_MV // w)),
                  pl.BlockSpec((L, w), lambda b_, c: (rows(b_, c), COL_MO // w)),
                  pl.BlockSpec((L, LANES), lambda b_, c: (rows(b_, c), 0)),
                  pl.BlockSpec((1, LANES), lambda b_, c: (0, 0)),
                  pl.BlockSpec((1, w), lambda b_, c: (0, 0))],
        out_specs=pl.BlockSpec((L, w), lambda b_, c: (rows(b_, c), 0)),
        out_shape=jax.ShapeDtypeStruct((batch * seq, w), BF16),
        scratch_shapes=[pltpu.VMEM((M_HEADS, M_HEAD_DIM, M_HEAD_DIM), F32),
                        pltpu.VMEM((M_HEADS, 8, M_HEAD_DIM), F32),
                        pltpu.VMEM((8, LANES), F32)],
        compiler_params=_params("arbitrary", "arbitrary"),
        name="mlstm",
    )(qk, qk, u, u, gates, gate_b, norm_g.reshape(1, w))


def _ssd_kernel(xbc_ref, z_ref, g_ref, dtb_ref, alog_ref, dskip_ref, ng_ref, y_ref, s_ref):
    c = pl.program_id(1)
    L = xbc_ref.shape[0]
    GW, N, P = S_GROUP_WIDTH, S_STATE, S_HEAD_DIM
    E = S_HEADS // S_GROUPS

    @pl.when(c == 0)
    def _():
        s_ref[...] = jnp.zeros_like(s_ref)

    dt = _softplus(g_ref[...] + dtb_ref[...])
    a = dt * (-jnp.exp(alog_ref[...]))
    row = lax.broadcasted_iota(jnp.int32, (L, L), 0)
    col = lax.broadcasted_iota(jnp.int32, (L, L), 1)
    causal = row >= col
    tril = jnp.where(causal, 1.0, 0.0).astype(F32)
    acum = _dot(tril, a, HIGHEST)
    acum_t = acum.T
    lane_p = lax.broadcasted_iota(jnp.int32, (L, LANES), 1)
    heads_per_slab = LANES // P
    for gi in range(S_GROUPS):
        gs = slice(gi * GW, (gi + 1) * GW)
        x = xbc_ref[:, gs].astype(F32)
        bm = xbc_ref[:, S_WIDTH + gi * N:S_WIDTH + (gi + 1) * N]
        cm = xbc_ref[:, S_WIDTH + S_GROUPS * N + gi * N:S_WIDTH + S_GROUPS * N + (gi + 1) * N]
        cb = _dot_nt(cm, bm)
        pieces, xdt_slabs, ac_slabs = [], [], []
        for pi in range(GW // LANES):
            x_slab = x[:, pi * LANES:(pi + 1) * LANES]
            dt_slab = ac_slab = None
            per_head = []
            for half in range(heads_per_slab):
                ln = GATE_LANE_SDT + gi * E + pi * heads_per_slab + half
                ac_col = jnp.broadcast_to(acum[:, ln:ln + 1], (L, LANES))
                dt_col = jnp.broadcast_to(dt[:, ln:ln + 1], (L, LANES))
                in_half = lane_p >= half * P
                dt_slab = dt_col if dt_slab is None else jnp.where(in_half, dt_col, dt_slab)
                ac_slab = ac_col if ac_slab is None else jnp.where(in_half, ac_col, ac_slab)
                seg = jnp.broadcast_to(acum[:, ln:ln + 1], (L, L)) - acum_t[ln:ln + 1, :]
                per_head.append(jnp.exp(jnp.where(causal, seg, NEG_INF)))
            xdt_slab = x_slab * dt_slab
            acc = None
            for half in range(heads_per_slab):
                in_half = (lane_p >= half * P) & (lane_p < (half + 1) * P)
                d = _dot((cb * per_head[half]).astype(BF16),
                         jnp.where(in_half, xdt_slab, 0.0).astype(BF16))
                acc = d if acc is None else acc + d
            pieces.append(acc)
            xdt_slabs.append(xdt_slab)
            ac_slabs.append(ac_slab)
        y = jnp.concatenate(pieces, axis=1)
        xdt = jnp.concatenate(xdt_slabs, axis=1)
        ac_x = jnp.concatenate(ac_slabs, axis=1)
        tot_x = ac_x[L - 1:L, :]
        state = s_ref[gi]
        y = y + _dot(cm, state.astype(BF16)) * jnp.exp(ac_x) + x * dskip_ref[:, gs]
        xdd = (xdt * jnp.exp(tot_x - ac_x)).astype(BF16)
        bm_t = bm.astype(F32).T.astype(BF16)
        s_ref[gi] = state * jnp.exp(tot_x) + _dot(bm_t, xdd)
        y = y * _silu(z_ref[:, gs].astype(F32))
        y_ref[:, gs] = _rms(y, ng_ref[:, gs]).astype(y_ref.dtype)


def ssd_mixer(u, gates, xbc, dt_bias, a_log, d_skip, norm_g, batch, seq):
    L = S_CHUNK
    nc = seq // L
    rows = lambda b_, c: b_ * nc + c
    return pl.pallas_call(
        _ssd_kernel,
        grid=(batch, nc),
        in_specs=[pl.BlockSpec((L, S_XBC), lambda b_, c: (rows(b_, c), 0)),
                  pl.BlockSpec((L, S_WIDTH), lambda b_, c: (rows(b_, c), COL_SZ // S_WIDTH)),
                  pl.BlockSpec((L, LANES), lambda b_, c: (rows(b_, c), 0)),
                  pl.BlockSpec((1, LANES), lambda b_, c: (0, 0)),
                  pl.BlockSpec((1, LANES), lambda b_, c: (0, 0)),
                  pl.BlockSpec((1, S_WIDTH), lambda b_, c: (0, 0)),
                  pl.BlockSpec((1, S_WIDTH), lambda b_, c: (0, 0))],
        out_specs=pl.BlockSpec((L, S_WIDTH), lambda b_, c: (rows(b_, c), 0)),
        out_shape=jax.ShapeDtypeStruct((batch * seq, S_WIDTH), BF16),
        scratch_shapes=[pltpu.VMEM((S_GROUPS, S_STATE, S_GROUP_WIDTH), F32)],
        compiler_params=_params("arbitrary", "arbitrary"),
        name="ssd",
    )(xbc, u, gates, dt_bias, a_log, d_skip.reshape(1, S_WIDTH), norm_g.reshape(1, S_WIDTH))


def _moba_kernel(q_ref, k_ref, v_ref, ex_ref, y_ref, kmean_ref):
    qi = pl.program_id(1)
    BS = A_BLOCK
    S = k_ref.shape[0]
    nb = S // BS
    nbp = kmean_ref.shape[1]
    H, Dh = A_HEADS, A_HEAD_DIM
    heads = [slice(h * Dh, (h + 1) * Dh) for h in range(H)]

    @pl.when(qi == 0)
    def _():
        for h in range(H):
            k = k_ref[:, heads[h]].astype(F32)
            k_mean = jnp.sum(k.reshape(nb, BS, Dh), axis=1) * (1.0 / BS)
            if nbp > nb:
                k_mean = jnp.concatenate([k_mean, jnp.zeros((nbp - nb, Dh), F32)], axis=0)
            kmean_ref[h] = k_mean

    blk = lax.broadcasted_iota(jnp.int32, (nbp, BS), 0)
    qbs, sels = [], []
    for h in range(H):
        q = q_ref[:, heads[h]].astype(F32) * (Dh ** -0.5)
        bs_t = _dot_nt(kmean_ref[h], q, HIGHEST)
        cnt = jnp.zeros((nbp, BS), F32)
        for j in range(nb):
            cj = bs_t[j:j + 1, :]
            beats = jnp.where(cj > bs_t, 1.0, jnp.where((cj == bs_t) & (blk > j), 1.0, 0.0))
            cnt = cnt + jnp.where(qi > j, beats, 0.0)
        sel_t = jnp.where((blk < qi) & (cnt < A_TOPK), 1.0, 0.0)
        sels.append(jnp.concatenate([sel_t, jnp.zeros((LANES - nbp, BS), F32)], axis=0).T.astype(BF16))
        qbs.append(q.astype(BF16))

    row = lax.broadcasted_iota(jnp.int32, (BS, BS), 0)
    col = lax.broadcasted_iota(jnp.int32, (BS, BS), 1)
    own = jnp.where(row >= col, 1.0, 0.0)
    for n in range(1, nb + 1):
        @pl.when(qi == n - 1)
        def _(n=n):
            w = n * BS
            for h in range(H):
                s = _dot_nt(qbs[h], k_ref[0:w, heads[h]])
                if n > 1:
                    allowed = jnp.concatenate([_dot(sels[h], ex_ref[:, 0:w - BS]), own], axis=1)
                else:
                    allowed = own
                s = jnp.where(allowed > 0.5, s, NEG_INF)
                m = jnp.max(s, axis=1, keepdims=True)
                p = jnp.exp(s - m)
                l = jnp.sum(p, axis=1, keepdims=True)
                y_ref[:, heads[h]] = (_dot(p.astype(BF16), v_ref[0:w, heads[h]]) / l).astype(y_ref.dtype)


def moba_mixer(u, batch, seq):
    nq = seq // A_BLOCK
    nbp = -(-nq // 8) * 8
    w = A_WIDTH
    expand = (jnp.arange(LANES, dtype=jnp.int32)[:, None]
              == jnp.arange(seq, dtype=jnp.int32)[None, :] // A_BLOCK).astype(BF16)
    return pl.pallas_call(
        _moba_kernel,
        grid=(batch, nq),
        in_specs=[pl.BlockSpec((A_BLOCK, w), lambda b_, i: (b_ * nq + i, COL_AQ // w)),
                  pl.BlockSpec((seq, w), lambda b_, i: (b_, COL_AK // w)),
                  pl.BlockSpec((seq, w), lambda b_, i: (b_, COL_AV // w)),
                  pl.BlockSpec((LANES, seq), lambda b_, i: (0, 0))],
        out_specs=pl.BlockSpec((A_BLOCK, w), lambda b_, i: (b_ * nq + i, 0)),
        out_shape=jax.ShapeDtypeStruct((batch * seq, w), BF16),
        scratch_shapes=[pltpu.VMEM((A_HEADS, nbp, A_HEAD_DIM), F32)],
        compiler_params=_params("arbitrary", "arbitrary"),
        name="moba",
    )(u, u, u, expand)


def _out_proj_body(ym_ref, ys_ref, ya_ref, x_ref, w_ref, g_ref):
    x = x_ref[...]
    x = x + _dot(ym_ref[...], w_ref[0:M_WIDTH, :])
    x = x + _dot(ys_ref[...], w_ref[M_WIDTH:M_WIDTH + S_WIDTH, :])
    x = x + _dot(ya_ref[...], w_ref[M_WIDTH + S_WIDTH:, :])
    return x, _rms(x, g_ref[...])


def _out_proj_kernel(ym_ref, ys_ref, ya_ref, x_ref, w_ref, g_ref, xo_ref, hn_ref):
    x, hn = _out_proj_body(ym_ref, ys_ref, ya_ref, x_ref, w_ref, g_ref)
    xo_ref[...] = x
    hn_ref[...] = hn.astype(hn_ref.dtype)


def _out_proj_route_kernel(ym_ref, ys_ref, ya_ref, x_ref, w_ref, g_ref, wr_ref, xo_ref, hn_ref, route_ref):
    x, hn = _out_proj_body(ym_ref, ys_ref, ya_ref, x_ref, w_ref, g_ref)
    xo_ref[...] = x
    hn_ref[...] = hn
    hn_hi = hn.astype(BF16)
    hn_lo = (hn - hn_hi.astype(F32)).astype(BF16)
    logits = (_dot(hn_hi, wr_ref[0]) + _dot(hn_lo, wr_ref[0])
              + _dot(hn_hi, wr_ref[1]))
    lane = lax.broadcasted_iota(jnp.int32, logits.shape, 1)
    logits = jnp.where(lane < N_EXPERTS, logits, NEG_INF)
    m1 = jnp.max(logits, axis=1, keepdims=True)
    i1 = jnp.min(jnp.where(logits == m1, lane, LANES), axis=1, keepdims=True)
    rest = jnp.where(lane == i1, NEG_INF, logits)
    m2 = jnp.max(rest, axis=1, keepdims=True)
    i2 = jnp.min(jnp.where(rest == m2, lane, LANES), axis=1, keepdims=True)
    e2 = jnp.exp(m2 - m1)
    w1 = 1.0 / (1.0 + e2)
    w2 = e2 / (1.0 + e2)
    route_ref[...] = jnp.where(lane == 0, i1.astype(F32),
                               jnp.where(lane == 1, i2.astype(F32),
                                         jnp.where(lane == 2, w1, jnp.where(lane == 3, w2, 0.0))))


def out_proj(ym, ys, ya, x, w, g, w_router=None):
    t, d = x.shape
    tm = ROW_TILE
    row = lambda i: (i, 0)
    fixed = lambda i: (0, 0)
    in_specs = [pl.BlockSpec((tm, M_WIDTH), row), pl.BlockSpec((tm, S_WIDTH), row),
                pl.BlockSpec((tm, A_WIDTH), row), pl.BlockSpec((tm, d), row),
                pl.BlockSpec((d, d), fixed), pl.BlockSpec((1, d), fixed)]
    args = [ym, ys, ya, x, w, g.reshape(1, d)]
    if w_router is None:
        kern = _out_proj_kernel
        out_specs = [pl.BlockSpec((tm, d), row), pl.BlockSpec((tm, d), row)]
        out_shape = [jax.ShapeDtypeStruct((t, d), F32), jax.ShapeDtypeStruct((t, d), BF16)]
    else:
        kern = _out_proj_route_kernel
        in_specs.append(pl.BlockSpec((2, d, LANES), lambda i: (0, 0, 0)))
        args.append(w_router)
        out_specs = [pl.BlockSpec((tm, d), row), pl.BlockSpec((tm, d), row),
                     pl.BlockSpec((tm, LANES), row)]
        out_shape = [jax.ShapeDtypeStruct((t, d), F32), jax.ShapeDtypeStruct((t, d), F32),
                     jax.ShapeDtypeStruct((t, LANES), F32)]
    return pl.pallas_call(
        kern, grid=(t // tm,), in_specs=in_specs, out_specs=out_specs, out_shape=out_shape,
        compiler_params=_params("arbitrary"), name="out_proj",
    )(*args)


def _ffn_kernel(hn_ref, wg_ref, wu_ref, wd_ref, o_ref):
    @pl.when(pl.program_id(1) == 0)
    def _():
        o_ref[...] = jnp.zeros_like(o_ref)

    hn = hn_ref[...]
    h = _silu(_dot(hn, wg_ref[...])) * _dot(hn, wu_ref[...])
    o_ref[...] += _dot(h.astype(BF16), wd_ref[...])


def dense_ffn(hn, wg, wu, wd):
    t, d = hn.shape
    ff = wg.shape[1]
    tm, tf = FFN_ROW_TILE, FF_TILE
    return pl.pallas_call(
        _ffn_kernel,
        grid=(t // tm, ff // tf),
        in_specs=[pl.BlockSpec((tm, d), lambda i, f: (i, 0)),
                  pl.BlockSpec((d, tf), lambda i, f: (0, f)),
                  pl.BlockSpec((d, tf), lambda i, f: (0, f)),
                  pl.BlockSpec((tf, d), lambda i, f: (f, 0))],
        out_specs=pl.BlockSpec((tm, d), lambda i, f: (i, 0)),
        out_shape=jax.ShapeDtypeStruct((t, d), F32),
        compiler_params=_params("arbitrary", "arbitrary"),
        name="dense_ffn",
    )(hn, wg, wu, wd)


def _expert_kernel(src_ref, te_ref, nu_ref, h_hbm, wg_ref, wu_ref, wd_ref, y_ref, gbuf_ref, xs_ref, sem):
    i = pl.program_id(0)
    f = pl.program_id(1)
    tm = y_ref.shape[0]
    n_used = nu_ref[0]
    used = i < n_used
    slot = lax.rem(i, 2)

    def row_copy(tile, r, slot_):
        return pltpu.make_async_copy(h_hbm.at[pl.ds(src_ref[tile * tm + r], 1)],
                                     gbuf_ref.at[slot_, pl.ds(r, 1)], sem.at[slot_])

    def start_tile(tile, slot_):
        def body(r, carry):
            row_copy(tile, r, slot_).start()
            return carry
        lax.fori_loop(0, tm, body, 0, unroll=DMA_UNROLL)

    def wait_tile(tile, slot_):
        def body(r, carry):
            row_copy(tile, r, slot_).wait()
            return carry
        lax.fori_loop(0, tm, body, 0, unroll=DMA_UNROLL)

    @pl.when(f == 0)
    def _():
        y_ref[...] = jnp.zeros_like(y_ref)

    @pl.when((f == 0) & used)
    def _():
        @pl.when(i == 0)
        def _():
            start_tile(0, 0)

        wait_tile(i, slot)
        xs_ref[...] = gbuf_ref[slot].astype(xs_ref.dtype)

        @pl.when(i + 1 < n_used)
        def _():
            start_tile(i + 1, 1 - slot)

    @pl.when(used)
    def _():
        xs = xs_ref[...]
        h = _silu(_dot(xs, wg_ref[0])) * _dot(xs, wu_ref[0])
        y_ref[...] += _dot(h.astype(BF16), wd_ref[0])


def expert_ffn(h, src, wg, wu, wd, tile_expert, n_used):
    t, d = h.shape
    r = src.shape[0]
    ff = wg.shape[2]
    tm, tf = MOE_ROW_TILE, MOE_FF_TILE
    nf = ff // tf

    def fblk(i, f, nu):
        return jnp.where(i < nu[0], f, nf - 1)

    return pl.pallas_call(
        _expert_kernel,
        grid_spec=pltpu.PrefetchScalarGridSpec(
            num_scalar_prefetch=3,
            grid=(r // tm, nf),
            in_specs=[pl.BlockSpec(memory_space=pl.ANY),
                      pl.BlockSpec((1, d, tf), lambda i, f, s, te, nu: (te[i], 0, fblk(i, f, nu))),
                      pl.BlockSpec((1, d, tf), lambda i, f, s, te, nu: (te[i], 0, fblk(i, f, nu))),
                      pl.BlockSpec((1, tf, d), lambda i, f, s, te, nu: (te[i], fblk(i, f, nu), 0))],
            out_specs=pl.BlockSpec((tm, d), lambda i, f, s, te, nu: (i, 0)),
            scratch_shapes=[pltpu.VMEM((2, tm, d), F32), pltpu.VMEM((tm, d), BF16),
                            pltpu.SemaphoreType.DMA((2,))]),
        out_shape=jax.ShapeDtypeStruct((r, d), F32),
        compiler_params=_params("arbitrary", "arbitrary"),
        name="moe_experts",
    )(src, tile_expert, n_used, h, wg, wu, wd)


def _combine_kernel(pos_ref, y_hbm, x_ref, route_ref, g_ref, xo_ref, hp_ref, buf_ref, sem):
    i = pl.program_id(0)
    n = pl.num_programs(0)
    tc = x_ref.shape[0]
    t_total = n * tc
    slot = lax.rem(i, 2)

    def copy(tile, r, choice, slot_):
        return pltpu.make_async_copy(y_hbm.at[pl.ds(pos_ref[choice * t_total + tile * tc + r], 1)],
                                     buf_ref.at[slot_, choice, pl.ds(r, 1)], sem.at[slot_])

    def start_tile(tile, slot_):
        def body(r, carry):
            copy(tile, r, 0, slot_).start(priority=0)
            copy(tile, r, 1, slot_).start(priority=1)
            return carry
        lax.fori_loop(0, tc, body, 0, unroll=DMA_UNROLL)

    def wait_tile(tile, slot_):
        def body(r, carry):
            copy(tile, r, 0, slot_).wait()
            copy(tile, r, 1, slot_).wait()
            return carry
        lax.fori_loop(0, tc, body, 0, unroll=DMA_UNROLL)

    @pl.when(i == 0)
    def _():
        start_tile(0, 0)

    @pl.when(i + 1 < n)
    def _():
        start_tile(i + 1, 1 - slot)

    wait_tile(i, slot)
    route = route_ref[...]
    x = x_ref[...] + route[:, 2:3] * buf_ref[slot, 0] + route[:, 3:4] * buf_ref[slot, 1]
    xo_ref[...] = x
    hp_ref[...] = _rms(x, g_ref[...]).astype(hp_ref.dtype)


def moe_combine(y, pos, x, route, g_next):
    t, d = x.shape
    tc = GATHER_TILE
    return pl.pallas_call(
        _combine_kernel,
        grid_spec=pltpu.PrefetchScalarGridSpec(
            num_scalar_prefetch=1,
            grid=(t // tc,),
            in_specs=[pl.BlockSpec(memory_space=pl.ANY),
                      pl.BlockSpec((tc, d), lambda i, p_: (i, 0)),
                      pl.BlockSpec((tc, LANES), lambda i, p_: (i, 0)),
                      pl.BlockSpec((1, d), lambda i, p_: (0, 0))],
            out_specs=[pl.BlockSpec((tc, d), lambda i, p_: (i, 0)),
                       pl.BlockSpec((tc, d), lambda i, p_: (i, 0))],
            scratch_shapes=[pltpu.VMEM((2, 2, tc, d), F32), pltpu.SemaphoreType.DMA((2,))]),
        out_shape=[jax.ShapeDtypeStruct((t, d), F32), jax.ShapeDtypeStruct((t, d), BF16)],
        compiler_params=_params("arbitrary"),
        name="moe_combine",
    )(pos, y, x, route, g_next.reshape(1, d))


def moe_ffn(hn, route, x, wg, wu, wd, g_next):
    t, d = x.shape
    tm = MOE_ROW_TILE
    n_tiles = (2 * t) // tm + N_EXPERTS
    e_flat = jnp.concatenate([route[:, 0], route[:, 1]]).astype(jnp.int32)
    onehot = (e_flat[:, None] == jnp.arange(N_EXPERTS, dtype=jnp.int32)[None, :]).astype(jnp.int32)
    counts = jnp.sum(onehot, axis=0)
    rank = jnp.sum((jnp.cumsum(onehot, axis=0) - onehot) * onehot, axis=1)
    padded = ((counts + tm - 1) // tm) * tm
    ends = jnp.cumsum(padded)
    offs = ends - padded
    pos = (jnp.sum(onehot * offs[None, :], axis=1) + rank).astype(jnp.int32)
    token = jnp.concatenate([jnp.arange(t, dtype=jnp.int32)] * 2)
    src = jnp.zeros((n_tiles * tm,), jnp.int32).at[pos].set(token)
    n_used = (ends[-1] // tm).astype(jnp.int32).reshape(1)
    tile_start = jnp.minimum(jnp.arange(n_tiles, dtype=jnp.int32), n_used[0] - 1) * tm
    tile_expert = jnp.sum((ends[None, :] <= tile_start[:, None]).astype(jnp.int32), axis=1)
    tile_expert = jnp.minimum(tile_expert, N_EXPERTS - 1).astype(jnp.int32)

    y = expert_ffn(hn, src, wg, wu, wd, tile_expert, n_used)
    return moe_combine(y, pos, x, route, g_next)


def _ple_kernel(pending_ffn, final, *refs):
    if pending_ffn:
        h_ref, gp_ref, p_ref, wg_ref, wp_ref, x_ref, g_ref, *outs = refs
        x = x_ref[...] + h_ref[...]
        hp = _rms(x, gp_ref[...]).astype(BF16)
    else:
        h_ref, p_ref, wg_ref, wp_ref, x_ref, g_ref, *outs = refs
        x = x_ref[...]
        hp = h_ref[...]
    gate = jax.nn.sigmoid(_dot(hp, wg_ref[...]))
    x = x + _dot(p_ref[...].astype(BF16), wp_ref[...]) * gate
    if final:
        outs[0][...] = _rms(x, g_ref[...])
    else:
        outs[0][...] = x
        outs[1][...] = _rms(x, g_ref[...]).astype(outs[1].dtype)


def ple(h, g_ple, p, wg, wp, x, g_next, final):
    t, d = x.shape
    dp = p.shape[1]
    tm = ROW_TILE
    row = lambda i: (i, 0)
    fixed = lambda i: (0, 0)
    pending_ffn = g_ple is not None
    in_specs = [pl.BlockSpec((tm, d), row)]
    args = [h]
    if pending_ffn:
        in_specs.append(pl.BlockSpec((1, d), fixed))
        args.append(g_ple.reshape(1, d))
    in_specs += [pl.BlockSpec((tm, dp), row), pl.BlockSpec((d, d), fixed), pl.BlockSpec((dp, d), fixed),
                 pl.BlockSpec((tm, d), row), pl.BlockSpec((1, d), fixed)]
    args += [p, wg, wp, x, g_next.reshape(1, d)]
    if final:
        out_specs = [pl.BlockSpec((tm, d), row)]
        out_shape = [jax.ShapeDtypeStruct((t, d), F32)]
    else:
        out_specs = [pl.BlockSpec((tm, d), row), pl.BlockSpec((tm, d), row)]
        out_shape = [jax.ShapeDtypeStruct((t, d), F32), jax.ShapeDtypeStruct((t, d), BF16)]
    return pl.pallas_call(
        functools.partial(_ple_kernel, pending_ffn, final),
        grid=(t // tm,), in_specs=in_specs, out_specs=out_specs, out_shape=out_shape,
        compiler_params=_params("arbitrary"), name="ple",
    )(*args)


def _pack_w_in(w):
    sizes = (M_WIDTH, M_WIDTH, M_WIDTH, M_WIDTH, M_HEADS, M_HEADS, S_WIDTH, S_XBC, S_HEADS,
             A_WIDTH, A_WIDTH, A_WIDTH)
    offs = [0]
    for s in sizes:
        offs.append(offs[-1] + s)
    seg = [w[:, offs[i]:offs[i + 1]] for i in range(len(sizes))]
    mq, mk, mv, mo, mi, mf, sz, sxbc, sdt, aq, ak, av = seg
    pad = jnp.zeros((w.shape[0], LANES - 2 * M_HEADS - S_HEADS), w.dtype)
    main = jnp.concatenate([mq, mk, mv, mo, sz, sxbc, aq, ak, av], axis=1).astype(BF16)
    gates = jnp.concatenate([mi, mf, sdt, pad], axis=1).astype(BF16)
    return main, gates


def _gate_lanes(vec, lane0):
    return jnp.zeros((1, LANES), F32).at[0, lane0:lane0 + vec.shape[0]].set(vec.astype(F32))


def kernel(x, p, ln_mix, w_in, w_out, m_conv_w, m_conv_b, m_gate_b, m_norm_g, s_conv_w, s_conv_b, s_dt_bias, s_a_log, s_d, s_norm_g, ln_ffn, ffn_w_gate, ffn_w_up, ffn_w_down, moe_router, moe_w_gate, moe_w_up, moe_w_down, ln_ple, ple_proj, ple_gate, ln_final):
    batch, seq, d = x.shape
    depth = w_in.shape[0]
    t = batch * seq
    xf = x.reshape(t, d)
    qk_scale = jnp.concatenate([jnp.ones((M_WIDTH,), F32),
                                jnp.full((M_WIDTH,), M_HEAD_DIM ** -0.5, F32)])
    xbc_scale = jnp.ones((S_XBC,), F32)
    xn = rmsnorm_bf16(xf, ln_mix[0])
    out = None
    for i in range(depth):
        w_main, w_gates = _pack_w_in(w_in[i])
        u = in_proj(xn, w_main, BF16, PROJ_COL_TILE)
        gates = in_proj(xn, w_gates, F32, LANES)
        qk = conv_silu(u, COL_MQ, 2 * M_WIDTH, m_conv_w[i], m_conv_b[i], qk_scale, batch, seq)
        xbc = conv_silu(u, COL_SXBC, S_XBC, s_conv_w[i], s_conv_b[i], xbc_scale, batch, seq)
        y_m = mlstm_mixer(u, gates, qk, _gate_lanes(m_gate_b[i], GATE_LANE_MI), m_norm_g[i], batch, seq)
        y_s = ssd_mixer(u, gates, xbc, _gate_lanes(s_dt_bias[i], GATE_LANE_SDT),
                        _gate_lanes(s_a_log[i], GATE_LANE_SDT),
                        jnp.repeat(s_d[i].astype(F32), S_HEAD_DIM), s_norm_g[i], batch, seq)
        y_a = moba_mixer(u, batch, seq)
        w_o = w_out[i].astype(BF16)
        j = i // 2
        if i % 2 == 0:
            xf, hn = out_proj(y_m, y_s, y_a, xf, w_o, ln_ffn[i])
            hp = dense_ffn(hn, ffn_w_gate[j].astype(BF16), ffn_w_up[j].astype(BF16),
                           ffn_w_down[j].astype(BF16))
            g_ple = ln_ple[i]
        else:
            w_r = jnp.zeros((d, LANES), F32).at[:, :N_EXPERTS].set(moe_router[j].astype(F32))
            w_r_hi = w_r.astype(BF16)
            w_r = jnp.stack([w_r_hi, (w_r - w_r_hi.astype(F32)).astype(BF16)])
            xf, hn, route = out_proj(y_m, y_s, y_a, xf, w_o, ln_ffn[i], w_r)
            xf, hp = moe_ffn(hn, route, xf, moe_w_gate[j].astype(BF16), moe_w_up[j].astype(BF16),
                             moe_w_down[j].astype(BF16), ln_ple[i])
            g_ple = None
        final = i == depth - 1
        g_next = ln_final if final else ln_mix[i + 1]
        res = ple(hp, g_ple, p[i].reshape(t, -1), ple_gate[i].astype(BF16), ple_proj[i].astype(BF16),
                  xf, g_next, final)
        if final:
            out = res[0]
        else:
            xf, xn = res
    return out.reshape(batch, seq, d)
```
